```python
import jax, jax.numpy as jnp
from jax import lax
import numpy as np

D_MODEL = 2048
BATCH = 4
SEQ = 2048
DEPTH = 2
DEC_BATCH = 128
DEC_SEQ = 1
PAST_LEN = 8192
PAGE_SIZE = 128

HEAD_DIM = 64
N_Q_HEADS = 16
N_KV_HEADS = 2
WINDOW = 128
C_ATT = N_Q_HEADS * HEAD_DIM
C_KV = N_KV_HEADS * HEAD_DIM
RWKV_HEADS = 16
RWKV_HEAD_SIZE = 64
C_RWKV = RWKV_HEADS * RWKV_HEAD_SIZE
W_LORA = 64
A_LORA = 64
G_LORA = 160
RWKV_COLS = 3 * C_RWKV + W_LORA + A_LORA + G_LORA
GN_EPS = 64e-5
POOL_WINDOWS = (2, 4, 8, 16)
POOL_GROUPS = len(POOL_WINDOWS)
C_POOL = 1024
POOL_GROUP_WIDTH = C_POOL // POOL_GROUPS
POOL_BUF = max(POOL_WINDOWS) - 1
N_BRANCH = 3
OFF_K = C_ATT
OFF_V = OFF_K + C_KV
OFF_RWKV = OFF_V + C_KV
OFF_POOL = OFF_RWKV + RWKV_COLS
OFF_GATE = OFF_POOL + C_POOL
IN_COLS = OFF_GATE + N_BRANCH * D_MODEL
N_KEYS = 128
N_EXPERTS = N_KEYS * N_KEYS
PEER_HEADS = 8
PEER_KEY_DIM = 256
PEER_HALF = PEER_KEY_DIM // 2
PEER_TOPK = 16
PEER_BLOCK = 128
NORM_EPS = 1e-6
MASK_VALUE = -1e30

kernel_name = 'hybrid_swa_rwkv7_pool_peer_step'


def _rmsnorm(x, g):
    xf = x.astype(jnp.float32)
    y = xf * lax.rsqrt(jnp.mean(xf * xf, axis=-1, keepdims=True) + NORM_EPS)
    return (y * g.astype(jnp.float32)).astype(x.dtype)


def _band_attend(q, k, v, q_pos, k_pos, sinks):
    B, Nb, Q = q.shape[:3]
    G = N_Q_HEADS // N_KV_HEADS
    qg = q.reshape(B, Nb, Q, N_KV_HEADS, G, HEAD_DIM)
    s = jnp.einsum('bnqhgd,bnkhd->bnhgqk', qg, k).astype(jnp.float32) * (HEAD_DIM ** -0.5)
    kp = k_pos[:, None, :]
    qp = q_pos[:, :, None]
    valid = (kp <= qp) & (kp >= qp - WINDOW) & (kp >= 0)
    s = jnp.where(valid[None, :, None, None], s, MASK_VALUE)
    sink = jnp.broadcast_to(sinks.astype(jnp.float32).reshape(1, 1, N_KV_HEADS, G, 1, 1),
                            s.shape[:-1] + (1,))
    p = jax.nn.softmax(jnp.concatenate([s, sink], axis=-1), axis=-1)[..., :-1]
    o = jnp.einsum('bnhgqk,bnkhd->bnqhgd', p.astype(v.dtype), v)
    return o.reshape(B, Nb, Q, N_Q_HEADS * HEAD_DIM)


def _attn_prompt(q, k, v, sinks):
    B, S = q.shape[:2]
    nb = S // WINDOW
    qb = q.reshape(B, nb, WINDOW, N_Q_HEADS, HEAD_DIM)
    kb = k.reshape(B, nb, WINDOW, N_KV_HEADS, HEAD_DIM)
    vb = v.reshape(B, nb, WINDOW, N_KV_HEADS, HEAD_DIM)
    pad = ((0, 0), (1, 0), (0, 0), (0, 0), (0, 0))
    kband = jnp.concatenate([jnp.pad(kb, pad)[:, :-1], kb], axis=2)
    vband = jnp.concatenate([jnp.pad(vb, pad)[:, :-1], vb], axis=2)
    q_pos = jnp.arange(S).reshape(nb, WINDOW)
    k_pos = (jnp.arange(nb)[:, None] - 1) * WINDOW + jnp.arange(2 * WINDOW)[None, :]
    o = _band_attend(qb, kband, vband, q_pos, k_pos, sinks)
    return o.reshape(B, S, C_ATT)


def _attn_sample(q, k, v, k_cache, v_cache, sinks, pos0):
    B, T = q.shape[:2]
    L = k_cache.shape[1]
    k_all = jnp.concatenate([k_cache.astype(k.dtype), k], axis=1)
    v_all = jnp.concatenate([v_cache.astype(v.dtype), v], axis=1)
    q_pos = (pos0 + jnp.arange(T))[None, :]
    k_pos = (pos0 - L + jnp.arange(L + T))[None, :]
    o = _band_attend(q[:, None], k_all[:, None], v_all[:, None], q_pos, k_pos, sinks)
    return o.reshape(B, T, C_ATT), k_all[:, -L:], v_all[:, -L:]


def _wkv_scan(r, w, k, v, kk, a, s0):
    def step(S, inp):
        r_t, w_t, k_t, v_t, kk_t, a_t = inp
        sa = jnp.einsum('bhij,bhj->bhi', S, -kk_t)
        S = (S * w_t[:, :, None, :] + sa[..., None] * (kk_t * a_t)[:, :, None, :]
             + v_t[..., None] * k_t[:, :, None, :])
        return S, jnp.einsum('bhij,bhj->bhi', S, r_t)
    xs = tuple(jnp.moveaxis(t, 1, 0) for t in (r, w, k, v, kk, a))
    S, ys = lax.scan(step, s0.astype(jnp.float32), xs)
    return jnp.moveaxis(ys, 0, 1), S


def _rwkv(p, shift0, wkv0, lp):
    B, T, _ = p.shape
    f32 = jnp.float32
    prev = jnp.concatenate([shift0[:, None].astype(p.dtype), p[:, :-1]], axis=1)
    xs = p + (prev - p) * lp['mu']
    o = 3 * C_RWKV
    r = xs[..., :C_RWKV]
    k = xs[..., C_RWKV:2 * C_RWKV]
    v = xs[..., 2 * C_RWKV:o]
    wd = xs[..., o:o + W_LORA]
    ad = xs[..., o + W_LORA:o + W_LORA + A_LORA]
    gd = xs[..., o + W_LORA + A_LORA:]
    w_log = -jax.nn.softplus(-(lp['w0'] + jnp.tanh(wd) @ lp['w2']).astype(f32)) - 0.5
    decay = jnp.exp(-jnp.exp(w_log))
    a = jax.nn.sigmoid((lp['a0'] + ad @ lp['a2']).astype(f32))
    g = (jax.nn.sigmoid(gd) @ lp['g2']).astype(f32)
    heads = lambda t: t.reshape(B, T, RWKV_HEADS, RWKV_HEAD_SIZE)
    kk = heads((k * lp['k_k']).astype(f32))
    kk = kk / jnp.maximum(jnp.sqrt(jnp.sum(kk * kk, axis=-1, keepdims=True)), 1e-12)
    k_mod = k.astype(f32) * (1.0 + (a - 1.0) * lp['k_a'].astype(f32))
    r_h, k_h, v_h = heads(r.astype(f32)), heads(k_mod), heads(v.astype(f32))
    y, S = _wkv_scan(r_h, heads(decay), k_h, v_h, kk, heads(a), wkv0)
    mean = jnp.mean(y, axis=-1, keepdims=True)
    var = jnp.mean(jnp.square(y - mean), axis=-1, keepdims=True)
    y = ((y - mean) * lax.rsqrt(var + GN_EPS)).reshape(B, T, C_RWKV)
    y = y * lp['ln_w'].astype(f32) + lp['ln_b'].astype(f32)
    bonus = jnp.sum(r_h * k_h * lp['r_k'].astype(f32), axis=-1, keepdims=True) * v_h
    y = (y + bonus.reshape(B, T, C_RWKV)) * g
    return y.astype(p.dtype), p[:, -1], S


def _pool(z, past, pos0, pool_w, pool_scale):
    B, T, C = z.shape
    ze = jnp.concatenate([past.astype(z.dtype), z], axis=1)
    cs = jnp.pad(jnp.cumsum(ze.astype(jnp.float32), axis=1), ((0, 0), (1, 0), (0, 0)))
    pos = pos0 + jnp.arange(T)
    outs = []
    for gi, w in enumerate(POOL_WINDOWS):
        sl = slice(gi * POOL_GROUP_WIDTH, (gi + 1) * POOL_GROUP_WIDTH)
        win = (cs[:, POOL_BUF + 1:POOL_BUF + T + 1, sl]
               - cs[:, POOL_BUF + 1 - w:POOL_BUF + T + 1 - w, sl])
        cnt = jnp.minimum(w, pos + 1).astype(jnp.float32)[None, :, None]
        outs.append(win / cnt - z[:, :, sl].astype(jnp.float32))
    d = jnp.stack(outs, axis=2).astype(z.dtype)
    y = jnp.einsum('btgc,gcd->btgd', d, pool_w).reshape(B, T, C) * pool_scale
    return y, ze[:, -POOL_BUF:]


def _peer(x, w_query, sub_keys, expert_u, expert_v):
    B, T, D = x.shape
    n = B * T
    nblk = -(-n // PEER_BLOCK)
    xt = jnp.pad(x.reshape(n, D), ((0, nblk * PEER_BLOCK - n), (0, 0))).reshape(nblk, PEER_BLOCK, D)
    KK = PEER_TOPK * PEER_TOPK

    def block(xb):
        q = (xb @ w_query).reshape(PEER_BLOCK, PEER_HEADS, 2, PEER_HALF)
        s = jnp.einsum('phcd,ckd->phck', q, sub_keys).astype(jnp.float32)
        top_s, top_i = lax.top_k(s, PEER_TOPK)
        cand_s = (top_s[:, :, 0, :, None] + top_s[:, :, 1, None, :]).reshape(PEER_BLOCK, PEER_HEADS, KK)
        cand_id = (top_i[:, :, 0, :, None] * N_KEYS + top_i[:, :, 1, None, :]).reshape(PEER_BLOCK, PEER_HEADS, KK)
        best_s, best_j = lax.top_k(cand_s, PEER_TOPK)
        ids = jnp.take_along_axis(cand_id, best_j, axis=-1)
        gate = jax.nn.softmax(best_s, axis=-1).astype(xb.dtype)
        u = jnp.take(expert_u, ids, axis=0)
        act = jax.nn.gelu(jnp.einsum('phkd,pd->phk', u, xb), approximate=False)
        vsel = jnp.take(expert_v, ids, axis=0)
        return jnp.einsum('phk,phkd->pd', gate * act, vsel)

    y = lax.map(block, xt)
    return y.reshape(nblk * PEER_BLOCK, D)[:n].reshape(B, T, D)


def _token_mix(xn, lp, state, pos0, prompt):
    B, T, _ = xn.shape
    k_cache, v_cache, wkv0, shift0, pool0 = state
    h = xn @ lp['w_in']
    q = h[..., :OFF_K].reshape(B, T, N_Q_HEADS, HEAD_DIM)
    k = h[..., OFF_K:OFF_V].reshape(B, T, N_KV_HEADS, HEAD_DIM)
    v = h[..., OFF_V:OFF_RWKV].reshape(B, T, N_KV_HEADS, HEAD_DIM)
    if prompt:
        y_att = _attn_prompt(q, k, v, lp['sinks'])
        new_k, new_v = k[:, -WINDOW:], v[:, -WINDOW:]
    else:
        y_att, new_k, new_v = _attn_sample(q, k, v, k_cache, v_cache, lp['sinks'], pos0)
    y_rwkv, new_shift, new_wkv = _rwkv(h[..., OFF_RWKV:OFF_POOL], shift0, wkv0, lp)
    y_pool, new_pool = _pool(h[..., OFF_POOL:OFF_GATE], pool0, pos0, lp['pool_w'], lp['pool_scale'])
    gates = jax.nn.sigmoid(h[..., OFF_GATE:] + lp['b_gate']).reshape(B, T, N_BRANCH, D_MODEL)
    merged = (gates[:, :, 0] * (y_att @ lp['w_att_o'])
              + gates[:, :, 1] * (y_rwkv @ lp['w_rwkv_o'])
              + gates[:, :, 2] * (y_pool @ lp['w_pool_o']))
    return merged @ lp['w_out'], (new_k, new_v, new_wkv, new_shift, new_pool)


def _layer(x, lp, state, pos0, prompt):
    mix, new_state = _token_mix(_rmsnorm(x, lp['norm_mix']), lp, state, pos0, prompt)
    x = x + mix
    x = x + _peer(_rmsnorm(x, lp['norm_ffn']), lp['peer_w_query'], lp['peer_sub_keys'],
                  lp['peer_u'], lp['peer_v'])
    return x, new_state


def setup_inputs(seed: int = 0) -> dict:
    key = jax.random.key(seed)
    ks = iter(jax.random.split(key, 40))

    def nrm(shape, scale=1.0):
        return scale * jax.random.normal(next(ks), shape, jnp.float32)

    def gain(shape):
        return 1.0 + nrm(shape, 0.02)

    win_rows = min(WINDOW, PAST_LEN)
    L = DEPTH
    return {
        'x_prompt': nrm((BATCH, SEQ, D_MODEL)),
        'x_sample': nrm((DEC_BATCH, DEC_SEQ, D_MODEL)),
        'cache_k': nrm((L, DEC_BATCH, win_rows, N_KV_HEADS, HEAD_DIM)),
        'cache_v': nrm((L, DEC_BATCH, win_rows, N_KV_HEADS, HEAD_DIM)),
        'state_wkv': nrm((L, DEC_BATCH, RWKV_HEADS, RWKV_HEAD_SIZE, RWKV_HEAD_SIZE), 0.5),
        'state_shift': nrm((L, DEC_BATCH, RWKV_COLS)),
        'state_pool': nrm((L, DEC_BATCH, POOL_BUF, C_POOL)),
        'norm_mix': gain((L, D_MODEL)),
        'w_in': nrm((L, D_MODEL, IN_COLS), D_MODEL ** -0.5),
        'b_gate': nrm((L, N_BRANCH * D_MODEL), 0.1),
        'attn_sinks': nrm((L, N_Q_HEADS), 0.5),
        'rwkv_mu': jax.random.uniform(next(ks), (L, RWKV_COLS), jnp.float32),
        'rwkv_w0': nrm((L, C_RWKV), 0.5),
        'rwkv_w2': nrm((L, W_LORA, C_RWKV), W_LORA ** -0.5),
        'rwkv_a0': nrm((L, C_RWKV), 0.1),
        'rwkv_a2': nrm((L, A_LORA, C_RWKV), A_LORA ** -0.5),
        'rwkv_g2': nrm((L, G_LORA, C_RWKV), G_LORA ** -0.5),
        'rwkv_k_k': 0.85 + nrm((L, C_RWKV), 0.05),
        'rwkv_k_a': 1.0 + nrm((L, C_RWKV), 0.05),
        'rwkv_r_k': nrm((L, RWKV_HEADS, RWKV_HEAD_SIZE), 0.1),
        'rwkv_ln_w': gain((L, C_RWKV)),
        'rwkv_ln_b': nrm((L, C_RWKV), 0.02),
        'pool_w': nrm((L, POOL_GROUPS, POOL_GROUP_WIDTH, POOL_GROUP_WIDTH), POOL_GROUP_WIDTH ** -0.5),
        'pool_scale': 1.0 + nrm((L, C_POOL), 0.1),
        'w_att_o': nrm((L, C_ATT, D_MODEL), C_ATT ** -0.5),
        'w_rwkv_o': nrm((L, C_RWKV, D_MODEL), C_RWKV ** -0.5),
        'w_pool_o': nrm((L, C_POOL, D_MODEL), C_POOL ** -0.5),
        'w_out': nrm((L, D_MODEL, D_MODEL), D_MODEL ** -0.5),
        'norm_ffn': gain((L, D_MODEL)),
        'peer_w_query': nrm((L, D_MODEL, PEER_HEADS * PEER_KEY_DIM), D_MODEL ** -0.5),
        'peer_sub_keys': nrm((L, 2, N_KEYS, PEER_HALF), PEER_HALF ** -0.5),
        'peer_u': nrm((L, N_EXPERTS, D_MODEL), D_MODEL ** -0.5),
        'peer_v': nrm((L, N_EXPERTS, D_MODEL), PEER_HEADS ** -0.5),
        'norm_final': gain((D_MODEL,)),
    }


def reference(x_prompt, x_sample, cache_k, cache_v, state_wkv, state_shift, state_pool,
              norm_mix, w_in, b_gate, attn_sinks, rwkv_mu, rwkv_w0, rwkv_w2, rwkv_a0, rwkv_a2,
              rwkv_g2, rwkv_k_k, rwkv_k_a, rwkv_r_k, rwkv_ln_w, rwkv_ln_b, pool_w, pool_scale,
              w_att_o, w_rwkv_o, w_pool_o, w_out, norm_ffn, peer_w_query, peer_sub_keys,
              peer_u, peer_v, norm_final):
    B = x_prompt.shape[0]
    xp, xs = x_prompt, x_sample
    new_p, new_s = [], []
    for l in range(DEPTH):
        lp = dict(norm_mix=norm_mix[l], w_in=w_in[l], b_gate=b_gate[l], sinks=attn_sinks[l],
                  mu=rwkv_mu[l], w0=rwkv_w0[l], w2=rwkv_w2[l], a0=rwkv_a0[l], a2=rwkv_a2[l],
                  g2=rwkv_g2[l], k_k=rwkv_k_k[l], k_a=rwkv_k_a[l], r_k=rwkv_r_k[l],
                  ln_w=rwkv_ln_w[l], ln_b=rwkv_ln_b[l], pool_w=pool_w[l], pool_scale=pool_scale[l],
                  w_att_o=w_att_o[l], w_rwkv_o=w_rwkv_o[l], w_pool_o=w_pool_o[l], w_out=w_out[l],
                  norm_ffn=norm_ffn[l], peer_w_query=peer_w_query[l],
                  peer_sub_keys=peer_sub_keys[l], peer_u=peer_u[l], peer_v=peer_v[l])
        p_state = (None, None,
                   jnp.zeros((B, RWKV_HEADS, RWKV_HEAD_SIZE, RWKV_HEAD_SIZE), jnp.float32),
                   jnp.zeros((B, RWKV_COLS), xp.dtype),
                   jnp.zeros((B, POOL_BUF, C_POOL), xp.dtype))
        xp, st_p = _layer(xp, lp, p_state, 0, True)
        s_state = (cache_k[l], cache_v[l], state_wkv[l], state_shift[l], state_pool[l])
        xs, st_s = _layer(xs, lp, s_state, PAST_LEN, False)
        new_p.append(st_p)
        new_s.append(st_s)
    y_prompt = _rmsnorm(xp, norm_final)
    y_sample = _rmsnorm(xs, norm_final)
    prompt_k = jnp.stack([st[0] for st in new_p])
    prompt_v = jnp.stack([st[1] for st in new_p])
    prompt_wkv = jnp.stack([st[2] for st in new_p])
    prompt_shift = jnp.stack([st[3] for st in new_p])
    prompt_pool = jnp.stack([st[4] for st in new_p])
    sample_k = jnp.stack([st[0] for st in new_s])
    sample_v = jnp.stack([st[1] for st in new_s])
    sample_wkv = jnp.stack([st[2] for st in new_s])
    sample_shift = jnp.stack([st[3] for st in new_s])
    sample_pool = jnp.stack([st[4] for st in new_s])
    return (y_prompt, y_sample, prompt_k, prompt_v, prompt_wkv, prompt_shift, prompt_pool,
            sample_k, sample_v, sample_wkv, sample_shift, sample_pool)
```

```python
import functools

import jax
import jax.numpy as jnp
from jax import lax
from jax.experimental import pallas as pl
from jax.experimental.pallas import tpu as pltpu

D_MODEL = 2048
BATCH = 4
SEQ = 2048
DEPTH = 2
DEC_BATCH = 128
PAST_LEN = 8192
HEAD_DIM = 64
N_Q_HEADS = 16
N_KV_HEADS = 2
Q_PER_KV = N_Q_HEADS // N_KV_HEADS
WINDOW = 128
C_ATT = N_Q_HEADS * HEAD_DIM
C_KV = N_KV_HEADS * HEAD_DIM
RWKV_HEADS = 16
RWKV_HEAD_SIZE = 64
C_RWKV = RWKV_HEADS * RWKV_HEAD_SIZE
W_LORA = 64
A_LORA = 64
G_LORA = 160
RWKV_COLS = 3 * C_RWKV + W_LORA + A_LORA + G_LORA
GN_EPS = 64e-5
POOL_WINDOWS = (2, 4, 8, 16)
POOL_GROUPS = len(POOL_WINDOWS)
C_POOL = 1024
POOL_GROUP_WIDTH = C_POOL // POOL_GROUPS
POOL_BUF = max(POOL_WINDOWS) - 1
N_BRANCH = 3
N_KEYS = 128
N_EXPERTS = N_KEYS * N_KEYS
PEER_HEADS = 8
PEER_HALF = 128
PEER_TOPK = 16
NORM_EPS = 1e-6
MASK_VALUE = -1e30

N_PROMPT = BATCH * SEQ
N_TOK = N_PROMPT + DEC_BATCH

OFF_K = C_ATT
OFF_V = OFF_K + C_KV
OFF_RWKV = OFF_V + C_KV
OFF_POOL = OFF_RWKV + RWKV_COLS
OFF_GATE = OFF_POOL + C_POOL

LORA_W = 512
P_GATE = 0
P_Q = P_GATE + N_BRANCH * D_MODEL
P_POOL = P_Q + C_ATT
P_R = P_POOL + C_POOL
P_K = P_R + C_RWKV
P_V = P_K + C_RWKV
P_LORA = P_V + C_RWKV
P_KATT = P_LORA + LORA_W
P_VATT = P_KATT + C_KV
P_COLS = 12288
RWKV_PACK = 3 * C_RWKV + LORA_W

TOK_TILE = 832
MIX_TILE = 128
PEER_TILE = 128
G_PITCH = 136
PREP_TILE = 208
EXPERT_TILE = 512
SCAN_CHUNK = 64

_VMEM_LIMIT = 56 * 1024 * 1024

_f32 = jnp.float32
_bf16 = jnp.bfloat16


def _params(sem, vmem=None):
    return pltpu.CompilerParams(dimension_semantics=sem, vmem_limit_bytes=vmem)


def _rmsnorm_kernel(x_ref, g_ref, o_ref):
    x = x_ref[...]
    y = x * lax.rsqrt(jnp.mean(x * x, axis=-1, keepdims=True) + NORM_EPS)
    o_ref[...] = (y * g_ref[...]).astype(o_ref.dtype)


def _rmsnorm(x, g, out_dtype, tile):
    n, d = x.shape
    return pl.pallas_call(
        _rmsnorm_kernel,
        grid=(n // tile,),
        in_specs=[pl.BlockSpec((tile, d), lambda i: (i, 0)),
                  pl.BlockSpec((1, d), lambda i: (0, 0))],
        out_specs=pl.BlockSpec((tile, d), lambda i: (i, 0)),
        out_shape=jax.ShapeDtypeStruct((n, d), out_dtype),
        compiler_params=_params(("parallel",), _VMEM_LIMIT),
        name="rmsnorm",
    )(x, g.reshape(1, d))


def _mm_kernel(a_ref, b_ref, o_ref):
    o_ref[...] = jnp.dot(a_ref[...], b_ref[...], preferred_element_type=_f32).astype(o_ref.dtype)


def _mm_res_kernel(a_ref, b_ref, r_ref, o_ref):
    o_ref[...] = r_ref[...] + jnp.dot(a_ref[...], b_ref[...], preferred_element_type=_f32)


def _matmul(a, b, tm, tn, out_dtype, name, res=None):
    m, k = a.shape
    n = b.shape[1]
    in_specs = [pl.BlockSpec((tm, k), lambda i, j: (i, 0)),
                pl.BlockSpec((k, tn), lambda i, j: (0, j))]
    args = [a, b]
    kern = _mm_kernel
    if res is not None:
        in_specs.append(pl.BlockSpec((tm, tn), lambda i, j: (i, j)))
        args.append(res)
        kern = _mm_res_kernel
    return pl.pallas_call(
        kern,
        grid=(m // tm, n // tn),
        in_specs=in_specs,
        out_specs=pl.BlockSpec((tm, tn), lambda i, j: (i, j)),
        out_shape=jax.ShapeDtypeStruct((m, n), out_dtype),
        compiler_params=_params(("parallel", "parallel"), _VMEM_LIMIT),
        name=name,
    )(*args)


def _attn_prompt_kernel(sink_ref, q_ref, kp_ref, kc_ref, vp_ref, vc_ref, o_ref):
    n = pl.program_id(1)
    q = q_ref[...].astype(_bf16)
    kband = jnp.concatenate([kp_ref[...], kc_ref[...]], axis=0).astype(_bf16)
    vband = jnp.concatenate([vp_ref[...], vc_ref[...]], axis=0).astype(_bf16)
    row = lax.broadcasted_iota(jnp.int32, (WINDOW, 2 * WINDOW), 0)
    col = lax.broadcasted_iota(jnp.int32, (WINDOW, 2 * WINDOW), 1)
    first_col = jnp.where(n > 0, 0, WINDOW)
    valid = (col <= row + WINDOW) & (col >= row) & (col >= first_col)
    outs = []
    for h in range(N_Q_HEADS):
        g = h // Q_PER_KV
        qh = q[:, h * HEAD_DIM:(h + 1) * HEAD_DIM]
        kg = kband[:, g * HEAD_DIM:(g + 1) * HEAD_DIM]
        vg = vband[:, g * HEAD_DIM:(g + 1) * HEAD_DIM]
        s = lax.dot_general(qh, kg, (((1,), (1,)), ((), ())), preferred_element_type=_f32)
        s = jnp.where(valid, s * (HEAD_DIM ** -0.5), MASK_VALUE)
        sink = sink_ref[h]
        m = jnp.maximum(jnp.max(s, axis=-1, keepdims=True), sink)
        p = jnp.exp(s - m)
        denom = jnp.sum(p, axis=-1, keepdims=True) + jnp.exp(sink - m)
        o = jnp.dot(p.astype(_bf16), vg, preferred_element_type=_f32)
        outs.append(o / denom)
    o_ref[...] = jnp.concatenate(outs, axis=-1).astype(o_ref.dtype)


def _attn_prompt(h, sinks):
    nb = SEQ // WINDOW
    qb, kb, vb = P_Q // C_ATT, P_KATT // C_KV, P_VATT // C_KV
    cur = lambda col: (lambda b, n: (b * nb + n, col))
    prev = lambda col: (lambda b, n: (b * nb + jnp.maximum(n - 1, 0), col))
    return pl.pallas_call(
        _attn_prompt_kernel,
        grid=(BATCH, nb),
        in_specs=[pl.BlockSpec(memory_space=pltpu.SMEM),
                  pl.BlockSpec((WINDOW, C_ATT), cur(qb)),
                  pl.BlockSpec((WINDOW, C_KV), prev(kb)),
                  pl.BlockSpec((WINDOW, C_KV), cur(kb)),
                  pl.BlockSpec((WINDOW, C_KV), prev(vb)),
                  pl.BlockSpec((WINDOW, C_KV), cur(vb))],
        out_specs=pl.BlockSpec((WINDOW, C_ATT), lambda b, n: (b * nb + n, 0)),
        out_shape=jax.ShapeDtypeStruct((N_PROMPT, C_ATT), _bf16),
        compiler_params=_params(("parallel", "parallel")),
        name="attn_prompt",
    )(sinks, h, h, h, h, h)


def _attn_sample_kernel(sink_ref, q_ref, kc_ref, vc_ref, kn_ref, vn_ref, o_ref):
    tb = q_ref.shape[0]
    q = q_ref[...]
    kc = kc_ref[...].astype(_bf16)
    vc = vc_ref[...].astype(_bf16)
    kn = kn_ref[...].astype(_bf16).astype(_f32)
    vn = vn_ref[...].astype(_bf16).astype(_f32)
    hidx = lax.broadcasted_iota(jnp.int32, (1, Q_PER_KV, 1), 1)
    outs = []
    for g in range(N_KV_HEADS):
        sl = slice(g * HEAD_DIM, (g + 1) * HEAD_DIM)
        qg = q[:, g * Q_PER_KV:(g + 1) * Q_PER_KV, :].astype(_bf16)
        s = jnp.einsum("bhd,bld->bhl", qg, kc[:, :, sl], preferred_element_type=_f32)
        s_new = jnp.sum(qg.astype(_f32) * kn[:, None, sl], axis=-1, keepdims=True)
        scale = HEAD_DIM ** -0.5
        s = s * scale
        s_new = s_new * scale
        sink = jnp.zeros((1, Q_PER_KV, 1), _f32)
        for j in range(Q_PER_KV):
            sink = jnp.where(hidx == j, sink_ref[g * Q_PER_KV + j], sink)
        m = jnp.maximum(jnp.maximum(jnp.max(s, axis=-1, keepdims=True), s_new), sink)
        p = jnp.exp(s - m)
        p_new = jnp.exp(s_new - m)
        denom = jnp.sum(p, axis=-1, keepdims=True) + p_new + jnp.exp(sink - m)
        o = jnp.einsum("bhl,bld->bhd", p.astype(_bf16), vc[:, :, sl], preferred_element_type=_f32)
        o = o + p_new.astype(_bf16).astype(_f32) * vn[:, None, sl]
        outs.append(o / denom)
    o_ref[...] = jnp.concatenate(outs, axis=1).astype(o_ref.dtype)


def _attn_sample(q3, kc, vc, kn, vn, sinks):
    tb = 16
    L = kc.shape[1]
    return pl.pallas_call(
        _attn_sample_kernel,
        grid=(DEC_BATCH // tb,),
        in_specs=[pl.BlockSpec(memory_space=pltpu.SMEM),
                  pl.BlockSpec((tb, N_Q_HEADS, HEAD_DIM), lambda i: (i, 0, 0)),
                  pl.BlockSpec((tb, L, C_KV), lambda i: (i, 0, 0)),
                  pl.BlockSpec((tb, L, C_KV), lambda i: (i, 0, 0)),
                  pl.BlockSpec((tb, C_KV), lambda i: (i, 0)),
                  pl.BlockSpec((tb, C_KV), lambda i: (i, 0))],
        out_specs=pl.BlockSpec((tb, N_Q_HEADS, HEAD_DIM), lambda i: (i, 0, 0)),
        out_shape=jax.ShapeDtypeStruct((DEC_BATCH, N_Q_HEADS, HEAD_DIM), _bf16),
        compiler_params=_params(("parallel",)),
        name="attn_sample",
    )(sinks, q3, kc, vc, kn, vn)


def _pool_project(d_groups, pw_ref, scale_ref, o_ref):
    outs = []
    for gi in range(POOL_GROUPS):
        outs.append(jnp.dot(d_groups[gi].astype(_bf16), pw_ref[gi], preferred_element_type=_f32))
    o_ref[...] = (jnp.concatenate(outs, axis=-1) * scale_ref[...]).astype(o_ref.dtype)


def _pool_prompt_kernel(zp_ref, zc_ref, pw_ref, scale_ref, o_ref, ze_scr):
    n = pl.program_id(1)
    tt = zc_ref.shape[0]
    halo = 16
    prev_tail = zp_ref[pl.ds(tt - halo, halo), :]
    keep_rows = jnp.where(n > 0, halo, 0)
    ze_scr[pl.ds(0, halo), :] = jnp.where(
        lax.broadcasted_iota(jnp.int32, prev_tail.shape, 0) < keep_rows, prev_tail, 0.0)
    ze_scr[pl.ds(halo, tt), :] = zc_ref[...]
    pos = n * tt + lax.broadcasted_iota(jnp.int32, (tt, 1), 0)
    d_groups = []
    for gi, w in enumerate(POOL_WINDOWS):
        cs = slice(gi * POOL_GROUP_WIDTH, (gi + 1) * POOL_GROUP_WIDTH)
        z = ze_scr[pl.ds(halo, tt), cs]
        win = z
        for s in range(1, w):
            win = win + ze_scr[pl.ds(halo - s, tt), cs]
        cnt = jnp.minimum(w, pos + 1).astype(_f32)
        d_groups.append(win / cnt - z)
    _pool_project(d_groups, pw_ref, scale_ref, o_ref)


def _pool_prompt(h, pool_w, pool_scale):
    tt = MIX_TILE
    nt = SEQ // tt
    cb = P_POOL // C_POOL
    return pl.pallas_call(
        _pool_prompt_kernel,
        grid=(BATCH, nt),
        in_specs=[pl.BlockSpec((tt, C_POOL), lambda b, n: (b * nt + jnp.maximum(n - 1, 0), cb)),
                  pl.BlockSpec((tt, C_POOL), lambda b, n: (b * nt + n, cb)),
                  pl.BlockSpec((POOL_GROUPS, POOL_GROUP_WIDTH, POOL_GROUP_WIDTH), lambda b, n: (0, 0, 0)),
                  pl.BlockSpec((1, C_POOL), lambda b, n: (0, 0))],
        out_specs=pl.BlockSpec((tt, C_POOL), lambda b, n: (b * nt + n, 0)),
        out_shape=jax.ShapeDtypeStruct((N_PROMPT, C_POOL), _bf16),
        scratch_shapes=[pltpu.VMEM((tt + 16, C_POOL), _f32)],
        compiler_params=_params(("parallel", "parallel")),
        name="pool_prompt",
    )(h, h, pool_w, pool_scale)


def _pool_sample_kernel(past_ref, z_ref, pw_ref, scale_ref, o_ref):
    z = z_ref[...]
    d_groups = []
    for gi, w in enumerate(POOL_WINDOWS):
        cs = slice(gi * POOL_GROUP_WIDTH, (gi + 1) * POOL_GROUP_WIDTH)
        zg = z[:, cs]
        win = zg
        for s in range(1, w):
            win = win + past_ref[POOL_BUF - s][:, cs]
        d_groups.append(win / float(w) - zg)
    _pool_project(d_groups, pw_ref, scale_ref, o_ref)


def _pool_sample(past_t, z, pool_w, pool_scale):
    vm = pl.BlockSpec(memory_space=pltpu.VMEM)
    return pl.pallas_call(
        _pool_sample_kernel,
        in_specs=[vm, vm, vm, vm],
        out_specs=vm,
        out_shape=jax.ShapeDtypeStruct((DEC_BATCH, C_POOL), _bf16),
        name="pool_sample",
    )(past_t, z, pool_w, pool_scale)


def _head_sums(x):
    outs = []
    for hh in range(RWKV_HEADS):
        xs = x[:, hh * RWKV_HEAD_SIZE:(hh + 1) * RWKV_HEAD_SIZE]
        outs.append(jnp.broadcast_to(jnp.sum(xs, axis=-1, keepdims=True), xs.shape))
    return jnp.concatenate(outs, axis=-1)


def _rwkv_prep_kernel(pr_ref, pk_ref, pv_ref, pl_ref, qr_ref, qk_ref, qv_ref, ql_ref,
                      mu_ref, w0_ref, a0_ref, kk_w_ref, ka_ref, w2_ref, a2_ref, g2_ref,
                      r_ref, w_ref, k_ref, v_ref, kk_ref, b_ref, g_ref):
    def shift(p_ref, q_ref, off, width):
        p = p_ref[...]
        return p + (q_ref[...] - p) * mu_ref[:, off:off + width]

    r = shift(pr_ref, qr_ref, 0, C_RWKV)
    k = shift(pk_ref, qk_ref, C_RWKV, C_RWKV)
    v = shift(pv_ref, qv_ref, 2 * C_RWKV, C_RWKV)
    lo = shift(pl_ref, ql_ref, 3 * C_RWKV, LORA_W)
    lane = lax.broadcasted_iota(jnp.int32, lo.shape, 1)
    f = jnp.where(lane < W_LORA, jnp.tanh(lo),
                  jnp.where(lane < W_LORA + A_LORA, lo, jax.nn.sigmoid(lo))).astype(_bf16)
    wpre = w0_ref[...] + jnp.dot(f, w2_ref[...], preferred_element_type=_f32)
    neg = -wpre
    softplus = jnp.maximum(neg, 0.0) + jnp.log(1.0 + jnp.exp(-jnp.abs(neg)))
    w_log = -softplus - 0.5
    decay = jnp.exp(-jnp.exp(w_log))
    a = jax.nn.sigmoid(a0_ref[...] + jnp.dot(f, a2_ref[...], preferred_element_type=_f32))
    g = jnp.dot(f, g2_ref[...], preferred_element_type=_f32)
    kk = k * kk_w_ref[...]
    kk = kk / jnp.maximum(jnp.sqrt(_head_sums(kk * kk)), 1e-12)
    r_ref[...] = r
    w_ref[...] = decay
    k_ref[...] = k * (1.0 + (a - 1.0) * ka_ref[...])
    v_ref[...] = v
    kk_ref[...] = kk
    b_ref[...] = kk * a
    g_ref[...] = g


def _rwkv_prep(h, hprev, mu, w0, a0, k_k, k_a, w2p, a2p, g2p):
    tt = PREP_TILE
    n = h.shape[0]
    cb = P_R // C_RWKV
    row = lambda c: (lambda i: (i, c))
    const = lambda i: (0, 0)
    big = lambda c: pl.BlockSpec((tt, C_RWKV), row(c))
    vec = pl.BlockSpec((1, C_RWKV), const)
    lw = pl.BlockSpec((LORA_W, C_RWKV), const)
    out = jax.ShapeDtypeStruct((n, C_RWKV), _f32)
    return pl.pallas_call(
        _rwkv_prep_kernel,
        grid=(n // tt,),
        in_specs=[big(cb), big(cb + 1), big(cb + 2), pl.BlockSpec((tt, LORA_W), row(P_LORA // LORA_W)),
                  big(0), big(1), big(2), pl.BlockSpec((tt, LORA_W), row(3 * C_RWKV // LORA_W)),
                  pl.BlockSpec((1, RWKV_PACK), const), vec, vec, vec, vec, lw, lw, lw],
        out_specs=[pl.BlockSpec((tt, C_RWKV), row(0))] * 7,
        out_shape=[out] * 7,
        compiler_params=_params(("parallel",), _VMEM_LIMIT),
        name="rwkv_prep",
    )(h, h, h, h, hprev, hprev, hprev, hprev, mu, w0, a0, k_k, k_a, w2p, a2p, g2p)


def _wkv_prompt_kernel(w_ref, kk_ref, b_ref, k_ref, r_ref, v_ref, y_ref, z_ref):
    @pl.when(pl.program_id(0) == 0)
    def _():
        z_ref[...] = jnp.zeros_like(z_ref)

    half = RWKV_HEAD_SIZE // 2

    def step(t, carry):
        w = w_ref[t]
        kk = kk_ref[t]
        bb = b_ref[t]
        k = k_ref[t]
        r = r_ref[t]
        for il in range(half):
            z = z_ref[il]
            sa = -jnp.sum(z * kk, axis=0, keepdims=True)
            z = z * w + sa * bb + v_ref[t, pl.ds(il, 1), :] * k
            z_ref[il] = z
            y_ref[t, pl.ds(il, 1), :] = jnp.sum(z * r, axis=0, keepdims=True)
        return carry

    lax.fori_loop(0, w_ref.shape[0], step, 0)


def _wkv_prompt(w, kk, b, k, r, v):
    t = w.shape[0]
    tc = SCAN_CHUNK
    half = RWKV_HEAD_SIZE // 2
    key = pl.BlockSpec((tc, RWKV_HEAD_SIZE, 128), lambda i: (i, 0, 0))
    val = pl.BlockSpec((tc, half, 128), lambda i: (i, 0, 0))
    return pl.pallas_call(
        _wkv_prompt_kernel,
        grid=(t // tc,),
        in_specs=[key] * 5 + [val],
        out_specs=[val, pl.BlockSpec((half, RWKV_HEAD_SIZE, 128), lambda i: (0, 0, 0))],
        out_shape=[jax.ShapeDtypeStruct((t, half, 128), _f32),
                   jax.ShapeDtypeStruct((half, RWKV_HEAD_SIZE, 128), _f32)],
        compiler_params=_params(("arbitrary",), _VMEM_LIMIT),
        name="wkv_prompt",
    )(w, kk, b, k, r, v)


def _wkv_sample_kernel(s_ref, w_ref, kk_ref, b_ref, k_ref, r_ref, v_ref, y_ref, so_ref):
    s = s_ref[...]
    row = lambda ref: ref[...]
    eye = (lax.broadcasted_iota(jnp.int32, (RWKV_HEAD_SIZE, RWKV_HEAD_SIZE), 0)
           == lax.broadcasted_iota(jnp.int32, (RWKV_HEAD_SIZE, RWKV_HEAD_SIZE), 1))
    sa = -jnp.sum(s * row(kk_ref), axis=-1, keepdims=True)
    vcol = jnp.sum(jnp.where(eye, row(v_ref), 0.0), axis=-1, keepdims=True)
    s = s * row(w_ref) + sa * row(b_ref) + vcol * row(k_ref)
    so_ref[...] = s
    ycol = jnp.sum(s * row(r_ref), axis=-1, keepdims=True)
    y_ref[...] = jnp.sum(jnp.where(eye, ycol, 0.0), axis=-2, keepdims=True)


def _wkv_sample(s0, w, kk, b, k, r, v):
    tb = 8
    st = pl.BlockSpec((tb, RWKV_HEADS, RWKV_HEAD_SIZE, RWKV_HEAD_SIZE), lambda i: (i, 0, 0, 0))
    vec = pl.BlockSpec((tb, RWKV_HEADS, 1, RWKV_HEAD_SIZE), lambda i: (i, 0, 0, 0))
    return pl.pallas_call(
        _wkv_sample_kernel,
        grid=(DEC_BATCH // tb,),
        in_specs=[st] + [vec] * 6,
        out_specs=[vec, st],
        out_shape=[jax.ShapeDtypeStruct((DEC_BATCH, RWKV_HEADS, 1, RWKV_HEAD_SIZE), _f32),
                   jax.ShapeDtypeStruct(s0.shape, _f32)],
        compiler_params=_params(("parallel",)),
        name="wkv_sample",
    )(s0, w, kk, b, k, r, v)


def _rwkv_post_kernel(y_ref, r_ref, k_ref, v_ref, g_ref, rk_ref, lw_ref, lb_ref, o_ref):
    y = y_ref[...]
    inv = 1.0 / RWKV_HEAD_SIZE
    mean = _head_sums(y) * inv
    c = y - mean
    var = _head_sums(c * c) * inv
    yn = c * lax.rsqrt(var + GN_EPS) * lw_ref[...] + lb_ref[...]
    bonus = _head_sums(r_ref[...] * k_ref[...] * rk_ref[...]) * v_ref[...]
    o_ref[...] = ((yn + bonus) * g_ref[...]).astype(o_ref.dtype)


def _rwkv_post(y, r, k, v, g, r_k, ln_w, ln_b):
    tt = PREP_TILE
    n = y.shape[0]
    big = pl.BlockSpec((tt, C_RWKV), lambda i: (i, 0))
    vec = pl.BlockSpec((1, C_RWKV), lambda i: (0, 0))
    return pl.pallas_call(
        _rwkv_post_kernel,
        grid=(n // tt,),
        in_specs=[big] * 5 + [vec] * 3,
        out_specs=big,
        out_shape=jax.ShapeDtypeStruct((n, C_RWKV), _bf16),
        compiler_params=_params(("parallel",), _VMEM_LIMIT),
        name="rwkv_post",
    )(y, r, k, v, g, r_k, ln_w, ln_b)


def _merge_kernel(ya_ref, yr_ref, yp_ref, wa_ref, wr_ref, wp_ref,
                  g0_ref, g1_ref, g2_ref, b0_ref, b1_ref, b2_ref, o_ref):
    acc = None
    for y_ref, w_ref, g_ref, b_ref in ((ya_ref, wa_ref, g0_ref, b0_ref),
                                       (yr_ref, wr_ref, g1_ref, b1_ref),
                                       (yp_ref, wp_ref, g2_ref, b2_ref)):
        term = jax.nn.sigmoid(g_ref[...] + b_ref[...]) * jnp.dot(
            y_ref[...], w_ref[...], preferred_element_type=_f32)
        acc = term if acc is None else acc + term
    o_ref[...] = acc.astype(o_ref.dtype)


def _merge(ya, yr, yp, wa, wr, wp, h, b_gate):
    tm, tn = TOK_TILE, 512
    nj = D_MODEL // tn
    ysp = pl.BlockSpec((tm, C_ATT), lambda i, j: (i, 0))
    wsp = pl.BlockSpec((C_ATT, tn), lambda i, j: (0, j))
    gsp = lambda br: pl.BlockSpec((tm, tn), lambda i, j: (i, br * nj + j))
    bsp = lambda br: pl.BlockSpec((1, tn), lambda i, j: (0, br * nj + j))
    return pl.pallas_call(
        _merge_kernel,
        grid=(N_TOK // tm, nj),
        in_specs=[ysp, ysp, ysp, wsp, wsp, wsp, gsp(0), gsp(1), gsp(2), bsp(0), bsp(1), bsp(2)],
        out_specs=pl.BlockSpec((tm, tn), lambda i, j: (i, j)),
        out_shape=jax.ShapeDtypeStruct((N_TOK, D_MODEL), _bf16),
        compiler_params=_params(("parallel", "parallel"), _VMEM_LIMIT),
        name="merge",
    )(ya, yr, yp, wa, wr, wp, h, h, h, b_gate, b_gate, b_gate)


def _peer_route_kernel(q_ref, keys_ref, i_ref, j_ref, g_ref,
                       t_scr, ti_scr, cand_scr, cid_scr, bi_scr, bj_scr, bg_scr):
    tp = q_ref.shape[0]
    q = q_ref[...]
    key_iota = lax.broadcasted_iota(jnp.int32, (N_KEYS, tp), 0)
    kk2 = PEER_TOPK * PEER_TOPK
    cand_iota = lax.broadcasted_iota(jnp.int32, (kk2, tp), 0)
    for hd in range(PEER_HEADS):
        for c in range(2):
            off = (hd * 2 + c) * PEER_HALF
            s = lax.dot_general(keys_ref[c], q[:, off:off + PEER_HALF],
                                (((1,), (1,)), ((), ())), preferred_element_type=_f32)
            for rk in range(PEER_TOPK):
                m = jnp.max(s, axis=0, keepdims=True)
                pos = jnp.min(jnp.where(s == m, key_iota, N_KEYS), axis=0, keepdims=True)
                t_scr[pl.ds(c * PEER_TOPK + rk, 1), :] = m
                ti_scr[pl.ds(c * PEER_TOPK + rk, 1), :] = pos
                s = jnp.where(key_iota == pos, -jnp.inf, s)
        t2 = t_scr[pl.ds(PEER_TOPK, PEER_TOPK), :]
        i2 = ti_scr[pl.ds(PEER_TOPK, PEER_TOPK), :]
        for r1 in range(PEER_TOPK):
            cand_scr[pl.ds(r1 * PEER_TOPK, PEER_TOPK), :] = t_scr[pl.ds(r1, 1), :] + t2
            cid_scr[pl.ds(r1 * PEER_TOPK, PEER_TOPK), :] = ti_scr[pl.ds(r1, 1), :] * N_KEYS + i2
        cand = cand_scr[...]
        cid = cid_scr[...]
        base = hd * PEER_TOPK
        for rk in range(PEER_TOPK):
            m = jnp.max(cand, axis=0, keepdims=True)
            pos = jnp.min(jnp.where(cand == m, cand_iota, kk2), axis=0, keepdims=True)
            sel = cand_iota == pos
            eid = jnp.max(jnp.where(sel, cid, -1), axis=0, keepdims=True)
            bg_scr[pl.ds(base + rk, 1), :] = m
            bi_scr[pl.ds(base + rk, 1), :] = eid >> 7
            bj_scr[pl.ds(base + rk, 1), :] = eid & (N_KEYS - 1)
            cand = jnp.where(sel, -jnp.inf, cand)
        best = bg_scr[pl.ds(base, PEER_TOPK), :]
        e = jnp.exp(best - best[0:1, :])
        bg_scr[pl.ds(base, PEER_TOPK), :] = e / jnp.sum(e, axis=0, keepdims=True)
    i_ref[...] = bi_scr[...].T
    j_ref[...] = bj_scr[...].T
    g_ref[...] = bg_scr[...].T


def _peer_route(q, keys):
    tp = PEER_TILE
    n = q.shape[0]
    slots = PEER_HEADS * PEER_TOPK
    osp = pl.BlockSpec((tp, slots), lambda i: (i, 0))
    return pl.pallas_call(
        _peer_route_kernel,
        grid=(n // tp,),
        in_specs=[pl.BlockSpec((tp, q.shape[1]), lambda i: (i, 0)),
                  pl.BlockSpec((2, N_KEYS, PEER_HALF), lambda i: (0, 0, 0))],
        out_specs=[osp, osp, osp],
        out_shape=[jax.ShapeDtypeStruct((n, slots), jnp.int32),
                   jax.ShapeDtypeStruct((n, slots), jnp.int32),
                   jax.ShapeDtypeStruct((n, slots), _f32)],
        scratch_shapes=[pltpu.VMEM((2 * PEER_TOPK, tp), _f32),
                        pltpu.VMEM((2 * PEER_TOPK, tp), jnp.int32),
                        pltpu.VMEM((PEER_TOPK * PEER_TOPK, tp), _f32),
                        pltpu.VMEM((PEER_TOPK * PEER_TOPK, tp), jnp.int32),
                        pltpu.VMEM((slots, tp), jnp.int32),
                        pltpu.VMEM((slots, tp), jnp.int32),
                        pltpu.VMEM((slots, tp), _f32)],
        compiler_params=_params(("parallel",)),
        name="peer_route",
    )(q, keys)


def _peer_gates_kernel(i_ref, j_ref, g_ref, o_ref, tile_scr):
    tp = i_ref.shape[0]
    slots = i_ref.shape[1]
    key_iota = lax.broadcasted_iota(jnp.int32, (N_KEYS, slots), 0)

    def per_token(p, carry):
        irow = i_ref[pl.ds(p, 1), :]
        jrow = j_ref[pl.ds(p, 1), :]
        grow = g_ref[pl.ds(p, 1), :]
        at = jnp.where(key_iota == irow, grow, 0.0).astype(_bf16)
        bt = jnp.where(key_iota == jrow, 1.0, 0.0).astype(_bf16)
        tile = lax.dot_general(at, bt, (((1,), (1,)), ((), ())), preferred_element_type=_f32)
        tile_scr[pl.ds(p * G_PITCH, N_KEYS), :] = tile
        return carry

    lax.fori_loop(0, tp, per_token, 0)
    for i in range(N_KEYS):
        o_ref[:, i * N_KEYS:(i + 1) * N_KEYS] = tile_scr[pl.ds(i, tp, stride=G_PITCH), :].astype(o_ref.dtype)


def _peer_gates(isel, jsel, gate):
    tp = PEER_TILE
    n, slots = isel.shape
    isp = pl.BlockSpec((tp, slots), lambda i: (i, 0))
    return pl.pallas_call(
        _peer_gates_kernel,
        grid=(n // tp,),
        in_specs=[isp, isp, isp],
        out_specs=pl.BlockSpec((tp, N_EXPERTS), lambda i: (i, 0)),
        out_shape=jax.ShapeDtypeStruct((n, N_EXPERTS), _bf16),
        scratch_shapes=[pltpu.VMEM((tp * G_PITCH, N_KEYS), _f32)],
        compiler_params=_params(("parallel",), _VMEM_LIMIT),
        name="peer_gates",
    )(isel, jsel, gate)


def _peer_dense_kernel(x_ref, u_ref, v_ref, g_ref, res_ref, o_ref):
    @pl.when(pl.program_id(1) == 0)
    def _():
        o_ref[...] = res_ref[...]

    s = lax.dot_general(x_ref[...], u_ref[...], (((1,), (1,)), ((), ())), preferred_element_type=_f32)
    act = 0.5 * s * (1.0 + lax.erf(s * (2.0 ** -0.5)))
    wgt = (g_ref[...].astype(_f32) * act).astype(_bf16)
    o_ref[...] += jnp.dot(wgt, v_ref[...], preferred_element_type=_f32)


def _peer_dense(xn, u, v, gates, res):
    tm, te = TOK_TILE, EXPERT_TILE
    n, d = xn.shape
    return pl.pallas_call(
        _peer_dense_kernel,
        grid=(n // tm, N_EXPERTS // te),
        in_specs=[pl.BlockSpec((tm, d), lambda i, e: (i, 0)),
                  pl.BlockSpec((te, d), lambda i, e: (e, 0)),
                  pl.BlockSpec((te, d), lambda i, e: (e, 0)),
                  pl.BlockSpec((tm, te), lambda i, e: (i, e)),
                  pl.BlockSpec((tm, d), lambda i, e: (i, 0))],
        out_specs=pl.BlockSpec((tm, d), lambda i, e: (i, 0)),
        out_shape=jax.ShapeDtypeStruct((n, d), _f32),
        compiler_params=_params(("parallel", "arbitrary"), _VMEM_LIMIT),
        name="peer_dense",
    )(xn, u, v, gates, res)


def _pack_w_in(w):
    seg = lambda a, b: w[:, a:b]
    lora = seg(OFF_RWKV + 3 * C_RWKV, OFF_POOL)
    pad = lambda x, width: jnp.pad(x, ((0, 0), (0, width - x.shape[1])))
    packed = jnp.concatenate([
        seg(OFF_GATE, OFF_GATE + N_BRANCH * D_MODEL),
        seg(0, OFF_K),
        seg(OFF_POOL, OFF_GATE),
        seg(OFF_RWKV, OFF_RWKV + 3 * C_RWKV),
        pad(lora, LORA_W),
        seg(OFF_K, OFF_V),
        seg(OFF_V, OFF_RWKV),
    ], axis=1)
    return pad(packed, P_COLS).astype(_bf16)


def _pad_rows(w, rows, at):
    return jnp.zeros((rows, w.shape[1]), w.dtype).at[at:at + w.shape[0]].set(w).astype(_bf16)


def _to_key_tiles(x):
    t = x.reshape(BATCH, SEQ, RWKV_HEADS, RWKV_HEAD_SIZE).transpose(1, 3, 0, 2)
    t = t.reshape(SEQ, RWKV_HEAD_SIZE, BATCH * RWKV_HEADS)
    return jnp.concatenate([t, t], axis=-1)


def _to_value_rows(x):
    half = RWKV_HEAD_SIZE // 2
    t = x.reshape(BATCH, SEQ, RWKV_HEADS, 2, half).transpose(1, 4, 3, 0, 2)
    return t.reshape(SEQ, half, 2 * BATCH * RWKV_HEADS)


def _from_value_rows(y):
    half = RWKV_HEAD_SIZE // 2
    t = y.reshape(SEQ, half, 2, BATCH, RWKV_HEADS).transpose(3, 0, 4, 2, 1)
    return t.reshape(N_PROMPT, C_RWKV)


def _state_from_lanes(z):
    half = RWKV_HEAD_SIZE // 2
    t = z.reshape(half, RWKV_HEAD_SIZE, 2, BATCH, RWKV_HEADS).transpose(3, 4, 2, 0, 1)
    return t.reshape(BATCH, RWKV_HEADS, RWKV_HEAD_SIZE, RWKV_HEAD_SIZE)


def _layer(x, lp, state):
    cache_k, cache_v, wkv0, shift0, pool0 = state
    row = lambda a: a.reshape(1, -1)

    xn = _rmsnorm(x, lp["norm_mix"], _bf16, TOK_TILE)
    h = _matmul(xn, lp["w_in"], TOK_TILE, 1024, _f32, "proj_in")
    hs = h[N_PROMPT:]

    ya_p = _attn_prompt(h, lp["sinks"])
    kn, vn = hs[:, P_KATT:P_KATT + C_KV], hs[:, P_VATT:P_VATT + C_KV]
    L = cache_k.shape[1]
    ya_s = _attn_sample(hs[:, P_Q:P_Q + C_ATT].reshape(DEC_BATCH, N_Q_HEADS, HEAD_DIM),
                        cache_k.reshape(DEC_BATCH, L, C_KV), cache_v.reshape(DEC_BATCH, L, C_KV),
                        kn, vn, lp["sinks"])
    ya = jnp.concatenate([ya_p, ya_s.reshape(DEC_BATCH, C_ATT)], axis=0)
    hp3 = h[:N_PROMPT].reshape(BATCH, SEQ, P_COLS)
    kv_tail = lambda off: hp3[:, SEQ - WINDOW:, off:off + C_KV].reshape(BATCH, WINDOW, N_KV_HEADS, HEAD_DIM)
    new_k_p, new_v_p = kv_tail(P_KATT), kv_tail(P_VATT)
    shape_kv = (DEC_BATCH, 1, N_KV_HEADS, HEAD_DIM)
    new_k_s = jnp.concatenate([cache_k, kn.reshape(shape_kv)], axis=1)[:, -L:]
    new_v_s = jnp.concatenate([cache_v, vn.reshape(shape_kv)], axis=1)[:, -L:]

    rw3 = hp3[:, :, P_R:P_R + RWKV_PACK]
    prev_p = jnp.concatenate([jnp.zeros((BATCH, 1, RWKV_PACK), _f32), rw3[:, :-1]], axis=1)
    shift_pad = jnp.pad(shift0, ((0, 0), (0, RWKV_PACK - RWKV_COLS)))
    hprev = jnp.concatenate([prev_p.reshape(N_PROMPT, RWKV_PACK), shift_pad], axis=0)
    r, w, k, v, kk, b, g = _rwkv_prep(h, hprev, lp["mu"], row(lp["w0"]), row(lp["a0"]), row(lp["k_k"]),
                                      row(lp["k_a"]), lp["w2"], lp["a2"], lp["g2"])
    pr = lambda a: a[:N_PROMPT].reshape(BATCH, SEQ, C_RWKV)
    y_lanes, z_lanes = _wkv_prompt(_to_key_tiles(pr(w)), _to_key_tiles(pr(kk)), _to_key_tiles(pr(b)),
                                   _to_key_tiles(pr(k)), _to_key_tiles(pr(r)), _to_value_rows(pr(v)))
    sm = lambda a: a[N_PROMPT:].reshape(DEC_BATCH, RWKV_HEADS, 1, RWKV_HEAD_SIZE)
    y_s, new_wkv_s = _wkv_sample(wkv0, sm(w), sm(kk), sm(b), sm(k), sm(r), sm(v))
    y = jnp.concatenate([_from_value_rows(y_lanes), y_s.reshape(DEC_BATCH, C_RWKV)], axis=0)
    yr = _rwkv_post(y, r, k, v, g, row(lp["r_k"]), row(lp["ln_w"]), row(lp["ln_b"]))
    new_wkv_p = _state_from_lanes(z_lanes)
    unpack_shift = lambda a: a[:, P_R:P_R + RWKV_COLS]
    new_shift_p = unpack_shift(hp3[:, -1])
    new_shift_s = unpack_shift(hs)

    yp_p = _pool_prompt(h, lp["pool_w"], row(lp["pool_scale"]))
    zs = hs[:, P_POOL:P_POOL + C_POOL]
    yp_s = _pool_sample(pool0.transpose(1, 0, 2), zs, lp["pool_w"], row(lp["pool_scale"]))
    yp = jnp.concatenate([yp_p, yp_s], axis=0)
    new_pool_p = hp3[:, SEQ - POOL_BUF:, P_POOL:P_POOL + C_POOL]
    new_pool_s = jnp.concatenate([pool0, zs[:, None]], axis=1)[:, -POOL_BUF:]

    merged = _merge(ya, yr, yp, lp["w_att_o"], lp["w_rwkv_o"], lp["w_pool_o"], h, row(lp["b_gate"]))
    x = _matmul(merged, lp["w_out"], TOK_TILE, 1024, _f32, "proj_out", res=x)

    xn2 = _rmsnorm(x, lp["norm_ffn"], _bf16, TOK_TILE)
    q = _matmul(xn2, lp["peer_w_query"], TOK_TILE, 1024, _bf16, "peer_query")
    isel, jsel, gate = _peer_route(q, lp["peer_sub_keys"])
    gates = _peer_gates(isel, jsel, gate)
    x = _peer_dense(xn2, lp["peer_u"], lp["peer_v"], gates, x)

    st_p = (new_k_p, new_v_p, new_wkv_p, new_shift_p, new_pool_p)
    st_s = (new_k_s, new_v_s, new_wkv_s, new_shift_s, new_pool_s)
    return x, st_p, st_s


def kernel(x_prompt, x_sample, cache_k, cache_v, state_wkv, state_shift, state_pool, norm_mix, w_in, b_gate, attn_sinks, rwkv_mu, rwkv_w0, rwkv_w2, rwkv_a0, rwkv_a2, rwkv_g2, rwkv_k_k, rwkv_k_a, rwkv_r_k, rwkv_ln_w, rwkv_ln_b, pool_w, pool_scale, w_att_o, w_rwkv_o, w_pool_o, w_out, norm_ffn, peer_w_query, peer_sub_keys, peer_u, peer_v, norm_final):
    x = jnp.concatenate([x_prompt.reshape(N_PROMPT, D_MODEL), x_sample.reshape(DEC_BATCH, D_MODEL)], axis=0)
    new_p, new_s = [], []
    bf = lambda a: a.astype(_bf16)
    for l in range(DEPTH):
        mu = rwkv_mu[l]
        mu_packed = jnp.pad(mu, (0, RWKV_PACK - RWKV_COLS)).reshape(1, RWKV_PACK)
        lp = dict(
            norm_mix=norm_mix[l], w_in=_pack_w_in(w_in[l]), b_gate=b_gate[l], sinks=attn_sinks[l],
            mu=mu_packed, w0=rwkv_w0[l], a0=rwkv_a0[l], k_k=rwkv_k_k[l], k_a=rwkv_k_a[l],
            w2=_pad_rows(rwkv_w2[l], LORA_W, 0), a2=_pad_rows(rwkv_a2[l], LORA_W, W_LORA),
            g2=_pad_rows(rwkv_g2[l], LORA_W, W_LORA + A_LORA),
            r_k=rwkv_r_k[l], ln_w=rwkv_ln_w[l], ln_b=rwkv_ln_b[l],
            pool_w=bf(pool_w[l]), pool_scale=pool_scale[l],
            w_att_o=bf(w_att_o[l]), w_rwkv_o=bf(w_rwkv_o[l]), w_pool_o=bf(w_pool_o[l]), w_out=bf(w_out[l]),
            norm_ffn=norm_ffn[l], peer_w_query=bf(peer_w_query[l]), peer_sub_keys=bf(peer_sub_keys[l]),
            peer_u=bf(peer_u[l]), peer_v=bf(peer_v[l]))
        state =(cache_k[l], cache_v[l], state_wkv[l], state_shift[l], state_pool[l])
        x, st_p, st_s = _layer(x, lp, state)
        new_p.append(st_p)
        new_s.append(st_s)
    y = _rmsnorm(x, norm_final, _f32, TOK_TILE)
    y_prompt = y[:N_PROMPT].reshape(BATCH, SEQ, D_MODEL)
    y_sample = y[N_PROMPT:].reshape(DEC_BATCH, 1, D_MODEL)
    stack = lambda sts, i: jnp.stack([st[i] for st in sts])
    return (y_prompt, y_sample,
            stack(new_p, 0), stack(new_p, 1), stack(new_p, 2), stack(new_p, 3), stack(new_p, 4),
            stack(new_s, 0), stack(new_s, 1), stack(new_s, 2), stack(new_s, 3), stack(new_s, 4))
```

```python
import jax
import jax.numpy as jnp
from jax import lax
from jax.experimental import pallas as pl
from jax.experimental.pallas import tpu as pltpu

D_MODEL = 2048
BATCH = 4
SEQ = 2048
DEPTH = 2
DEC_BATCH = 128
PAST_LEN = 8192
HEAD_DIM = 64
N_Q_HEADS = 16
N_KV_HEADS = 2
Q_PER_KV = N_Q_HEADS // N_KV_HEADS
WINDOW = 128
C_ATT = N_Q_HEADS * HEAD_DIM
C_KV = N_KV_HEADS * HEAD_DIM
RWKV_HEADS = 16
RWKV_HEAD_SIZE = 64
C_RWKV = RWKV_HEADS * RWKV_HEAD_SIZE
W_LORA = 64
A_LORA = 64
G_LORA = 160
RWKV_COLS = 3 * C_RWKV + W_LORA + A_LORA + G_LORA
GN_EPS = 64e-5
POOL_WINDOWS = (2, 4, 8, 16)
POOL_GROUPS = len(POOL_WINDOWS)
C_POOL = 1024
POOL_GROUP_WIDTH = C_POOL // POOL_GROUPS
POOL_BUF = max(POOL_WINDOWS) - 1
N_BRANCH = 3
N_KEYS = 128
N_EXPERTS = N_KEYS * N_KEYS
PEER_HEADS = 8
PEER_HALF = 128
PEER_TOPK = 16
NORM_EPS = 1e-6
MASK_VALUE = -1e30

N_PROMPT = BATCH * SEQ
N_TOK = N_PROMPT + DEC_BATCH

OFF_K = C_ATT
OFF_V = OFF_K + C_KV
OFF_RWKV = OFF_V + C_KV
OFF_POOL = OFF_RWKV + RWKV_COLS
OFF_GATE = OFF_POOL + C_POOL

LORA_W = 512
P_GATE = 0
P_Q = P_GATE + N_BRANCH * D_MODEL
P_POOL = P_Q + C_ATT
P_R = P_POOL + C_POOL
P_K = P_R + C_RWKV
P_V = P_K + C_RWKV
P_LORA = P_V + C_RWKV
P_KATT = P_LORA + LORA_W
P_VATT = P_KATT + C_KV
P_COLS = 12288
RWKV_PACK = 3 * C_RWKV + LORA_W

SUBLANES = 8
LANES = 128
TOK_TILE = 832
PROMPT_TILE = 1024
MIX_TILE = 128
PREP_TILE = 256
PEER_TILE = 128
G_PITCH = 136
EXPERT_TILE = 512
SCAN_CHUNK = 64

_VMEM_LIMIT = 56 * 1024 * 1024

_f32 = jnp.float32
_bf16 = jnp.bfloat16


def _params(sem, vmem=None):
    return pltpu.CompilerParams(dimension_semantics=sem, vmem_limit_bytes=vmem)


def _rmsnorm_kernel(x_ref, g_ref, o_ref):
    x = x_ref[...]
    y = x * lax.rsqrt(jnp.mean(x * x, axis=-1, keepdims=True) + NORM_EPS)
    o_ref[...] = (y * g_ref[...]).astype(o_ref.dtype)


def _rmsnorm(x, g, out_dtype, tile):
    n, d = x.shape
    return pl.pallas_call(
        _rmsnorm_kernel,
        grid=(n // tile,),
        in_specs=[pl.BlockSpec((tile, d), lambda i: (i, 0)),
                  pl.BlockSpec((1, d), lambda i: (0, 0))],
        out_specs=pl.BlockSpec((tile, d), lambda i: (i, 0)),
        out_shape=jax.ShapeDtypeStruct((n, d), out_dtype),
        compiler_params=_params(("parallel",), _VMEM_LIMIT),
        name="rmsnorm",
    )(x, g.reshape(1, d))


def _mm_kernel(a_ref, b_ref, o_ref):
    o_ref[...] = jnp.dot(a_ref[...], b_ref[...], preferred_element_type=_f32).astype(o_ref.dtype)


def _mm_res_kernel(a_ref, b_ref, r_ref, o_ref):
    o_ref[...] = r_ref[...] + jnp.dot(a_ref[...], b_ref[...], preferred_element_type=_f32)


def _matmul(a, b, tm, tn, out_dtype, name, res=None):
    m, k = a.shape
    n = b.shape[1]
    in_specs = [pl.BlockSpec((tm, k), lambda i, j: (i, 0)),
                pl.BlockSpec((k, tn), lambda i, j: (0, j))]
    args = [a, b]
    kern = _mm_kernel
    if res is not None:
        in_specs.append(pl.BlockSpec((tm, tn), lambda i, j: (i, j)))
        args.append(res)
        kern = _mm_res_kernel
    return pl.pallas_call(
        kern,
        grid=(m // tm, n // tn),
        in_specs=in_specs,
        out_specs=pl.BlockSpec((tm, tn), lambda i, j: (i, j)),
        out_shape=jax.ShapeDtypeStruct((m, n), out_dtype),
        compiler_params=_params(("parallel", "parallel"), _VMEM_LIMIT),
        name=name,
    )(*args)


def _attn_prompt_kernel(sink_ref, q_ref, kp_ref, kc_ref, vp_ref, vc_ref, o_ref):
    n = pl.program_id(1)
    q = q_ref[...].astype(_bf16)
    kband = jnp.concatenate([kp_ref[...], kc_ref[...]], axis=0).astype(_bf16)
    vband = jnp.concatenate([vp_ref[...], vc_ref[...]], axis=0).astype(_bf16)
    row = lax.broadcasted_iota(jnp.int32, (WINDOW, 2 * WINDOW), 0)
    col = lax.broadcasted_iota(jnp.int32, (WINDOW, 2 * WINDOW), 1)
    first_col = jnp.where(n > 0, 0, WINDOW)
    valid = (col <= row + WINDOW) & (col >= row) & (col >= first_col)
    outs = []
    for h in range(N_Q_HEADS):
        g = h // Q_PER_KV
        qh = q[:, h * HEAD_DIM:(h + 1) * HEAD_DIM]
        kg = kband[:, g * HEAD_DIM:(g + 1) * HEAD_DIM]
        vg = vband[:, g * HEAD_DIM:(g + 1) * HEAD_DIM]
        s = lax.dot_general(qh, kg, (((1,), (1,)), ((), ())), preferred_element_type=_f32)
        s = jnp.where(valid, s * (HEAD_DIM ** -0.5), MASK_VALUE)
        sink = sink_ref[h]
        m = jnp.maximum(jnp.max(s, axis=-1, keepdims=True), sink)
        p = jnp.exp(s - m)
        denom = jnp.sum(p, axis=-1, keepdims=True) + jnp.exp(sink - m)
        o = jnp.dot(p.astype(_bf16), vg, preferred_element_type=_f32)
        outs.append(o / denom)
    o_ref[...] = jnp.concatenate(outs, axis=-1).astype(o_ref.dtype)


def _attn_prompt(h, sinks):
    nb = SEQ // WINDOW
    qb, kb, vb = P_Q // C_ATT, P_KATT // C_KV, P_VATT // C_KV
    cur = lambda col: (lambda b, n: (b * nb + n, col))
    prev = lambda col: (lambda b, n: (b * nb + jnp.maximum(n - 1, 0), col))
    return pl.pallas_call(
        _attn_prompt_kernel,
        grid=(BATCH, nb),
        in_specs=[pl.BlockSpec(memory_space=pltpu.SMEM),
                  pl.BlockSpec((WINDOW, C_ATT), cur(qb)),
                  pl.BlockSpec((WINDOW, C_KV), prev(kb)),
                  pl.BlockSpec((WINDOW, C_KV), cur(kb)),
                  pl.BlockSpec((WINDOW, C_KV), prev(vb)),
                  pl.BlockSpec((WINDOW, C_KV), cur(vb))],
        out_specs=pl.BlockSpec((WINDOW, C_ATT), lambda b, n: (b * nb + n, 0)),
        out_shape=jax.ShapeDtypeStruct((N_PROMPT, C_ATT), _bf16),
        compiler_params=_params(("parallel", "parallel")),
        name="attn_prompt",
    )(sinks, h, h, h, h, h)


def _attn_sample_kernel(sink_ref, q_ref, kc_ref, vc_ref, kn_ref, vn_ref, o_ref):
    q = q_ref[...]
    kc = kc_ref[...].astype(_bf16)
    vc = vc_ref[...].astype(_bf16)
    kn = kn_ref[...].astype(_bf16).astype(_f32)
    vn = vn_ref[...].astype(_bf16).astype(_f32)
    hidx = lax.broadcasted_iota(jnp.int32, (1, Q_PER_KV, 1), 1)
    outs = []
    for g in range(N_KV_HEADS):
        sl = slice(g * HEAD_DIM, (g + 1) * HEAD_DIM)
        qg = q[:, g * Q_PER_KV:(g + 1) * Q_PER_KV, :].astype(_bf16)
        s = jnp.einsum("bhd,bld->bhl", qg, kc[:, :, sl], preferred_element_type=_f32)
        s_new = jnp.sum(qg.astype(_f32) * kn[:, None, sl], axis=-1, keepdims=True)
        scale = HEAD_DIM ** -0.5
        s = s * scale
        s_new = s_new * scale
        sink = jnp.zeros((1, Q_PER_KV, 1), _f32)
        for j in range(Q_PER_KV):
            sink = jnp.where(hidx == j, sink_ref[g * Q_PER_KV + j], sink)
        m = jnp.maximum(jnp.maximum(jnp.max(s, axis=-1, keepdims=True), s_new), sink)
        p = jnp.exp(s - m)
        p_new = jnp.exp(s_new - m)
        denom = jnp.sum(p, axis=-1, keepdims=True) + p_new + jnp.exp(sink - m)
        o = jnp.einsum("bhl,bld->bhd", p.astype(_bf16), vc[:, :, sl], preferred_element_type=_f32)
        o = o + p_new.astype(_bf16).astype(_f32) * vn[:, None, sl]
        outs.append(o / denom)
    o_ref[...] = jnp.concatenate(outs, axis=1).astype(o_ref.dtype)


def _attn_sample(q3, kc, vc, kn, vn, sinks):
    tb = 16
    L = kc.shape[1]
    return pl.pallas_call(
        _attn_sample_kernel,
        grid=(DEC_BATCH // tb,),
        in_specs=[pl.BlockSpec(memory_space=pltpu.SMEM),
                  pl.BlockSpec((tb, N_Q_HEADS, HEAD_DIM), lambda i: (i, 0, 0)),
                  pl.BlockSpec((tb, L, C_KV), lambda i: (i, 0, 0)),
                  pl.BlockSpec((tb, L, C_KV), lambda i: (i, 0, 0)),
                  pl.BlockSpec((tb, C_KV), lambda i: (i, 0)),
                  pl.BlockSpec((tb, C_KV), lambda i: (i, 0))],
        out_specs=pl.BlockSpec((tb, N_Q_HEADS, HEAD_DIM), lambda i: (i, 0, 0)),
        out_shape=jax.ShapeDtypeStruct((DEC_BATCH, N_Q_HEADS, HEAD_DIM), _bf16),
        compiler_params=_params(("parallel",)),
        name="attn_sample",
    )(sinks, q3, kc, vc, kn, vn)


def _pool_project(d_groups, pw_ref, scale_ref, o_ref):
    outs = []
    for gi in range(POOL_GROUPS):
        outs.append(jnp.dot(d_groups[gi].astype(_bf16), pw_ref[gi], preferred_element_type=_f32))
    o_ref[...] = (jnp.concatenate(outs, axis=-1) * scale_ref[...]).astype(o_ref.dtype)


def _pool_prompt_kernel(zp_ref, zc_ref, pw_ref, scale_ref, o_ref, ze_scr):
    n = pl.program_id(1)
    tt = zc_ref.shape[0]
    halo = 2 * SUBLANES
    prev_tail = zp_ref[pl.ds(tt - halo, halo), :]
    keep_rows = jnp.where(n > 0, halo, 0)
    ze_scr[pl.ds(0, halo), :] = jnp.where(
        lax.broadcasted_iota(jnp.int32, prev_tail.shape, 0) < keep_rows, prev_tail, 0.0)
    ze_scr[pl.ds(halo, tt), :] = zc_ref[...]
    pos = n * tt + lax.broadcasted_iota(jnp.int32, (tt, 1), 0)
    d_groups = []
    for gi, w in enumerate(POOL_WINDOWS):
        cs = slice(gi * POOL_GROUP_WIDTH, (gi + 1) * POOL_GROUP_WIDTH)
        z = ze_scr[pl.ds(halo, tt), cs]
        win = z
        for s in range(1, w):
            win = win + ze_scr[pl.ds(halo - s, tt), cs]
        cnt = jnp.minimum(w, pos + 1).astype(_f32)
        d_groups.append(win / cnt - z)
    _pool_project(d_groups, pw_ref, scale_ref, o_ref)


def _pool_prompt(h, pool_w, pool_scale):
    tt = MIX_TILE
    nt = SEQ // tt
    cb = P_POOL // C_POOL
    return pl.pallas_call(
        _pool_prompt_kernel,
        grid=(BATCH, nt),
        in_specs=[pl.BlockSpec((tt, C_POOL), lambda b, n: (b * nt + jnp.maximum(n - 1, 0), cb)),
                  pl.BlockSpec((tt, C_POOL), lambda b, n: (b * nt + n, cb)),
                  pl.BlockSpec((POOL_GROUPS, POOL_GROUP_WIDTH, POOL_GROUP_WIDTH), lambda b, n: (0, 0, 0)),
                  pl.BlockSpec((1, C_POOL), lambda b, n: (0, 0))],
        out_specs=pl.BlockSpec((tt, C_POOL), lambda b, n: (b * nt + n, 0)),
        out_shape=jax.ShapeDtypeStruct((N_PROMPT, C_POOL), _bf16),
        scratch_shapes=[pltpu.VMEM((tt + 2 * SUBLANES, C_POOL), _f32)],
        compiler_params=_params(("parallel", "parallel")),
        name="pool_prompt",
    )(h, h, pool_w, pool_scale)


def _pool_sample_kernel(past_ref, z_ref, pw_ref, scale_ref, o_ref):
    z = z_ref[...]
    d_groups = []
    for gi, w in enumerate(POOL_WINDOWS):
        cs = slice(gi * POOL_GROUP_WIDTH, (gi + 1) * POOL_GROUP_WIDTH)
        zg = z[:, cs]
        win = zg
        for s in range(1, w):
            win = win + past_ref[POOL_BUF - s][:, cs]
        d_groups.append(win / float(w) - zg)
    _pool_project(d_groups, pw_ref, scale_ref, o_ref)


def _pool_sample(past_t, z, pool_w, pool_scale):
    vm = pl.BlockSpec(memory_space=pltpu.VMEM)
    return pl.pallas_call(
        _pool_sample_kernel,
        in_specs=[vm, vm, vm, vm],
        out_specs=vm,
        out_shape=jax.ShapeDtypeStruct((DEC_BATCH, C_POOL), _bf16),
        name="pool_sample",
    )(past_t, z, pool_w, pool_scale)


def _head_sums(x):
    outs = []
    for hh in range(RWKV_HEADS):
        xs = x[:, hh * RWKV_HEAD_SIZE:(hh + 1) * RWKV_HEAD_SIZE]
        outs.append(jnp.broadcast_to(jnp.sum(xs, axis=-1, keepdims=True), xs.shape))
    return jnp.concatenate(outs, axis=-1)


def _rwkv_prep_math(r, k, v, lo, w0_ref, a0_ref, kk_w_ref, ka_ref, w2_ref, a2_ref, g2_ref, outs):
    r_ref, w_ref, k_ref, v_ref, kk_ref, b_ref, g_ref = outs
    lane = lax.broadcasted_iota(jnp.int32, lo.shape, 1)
    f = jnp.where(lane < W_LORA, jnp.tanh(lo),
                  jnp.where(lane < W_LORA + A_LORA, lo, jax.nn.sigmoid(lo))).astype(_bf16)
    wpre = w0_ref[...] + jnp.dot(f, w2_ref[...], preferred_element_type=_f32)
    neg = -wpre
    softplus = jnp.maximum(neg, 0.0) + jnp.log(1.0 + jnp.exp(-jnp.abs(neg)))
    w_log = -softplus - 0.5
    decay = jnp.exp(-jnp.exp(w_log))
    a = jax.nn.sigmoid(a0_ref[...] + jnp.dot(f, a2_ref[...], preferred_element_type=_f32))
    g = jnp.dot(f, g2_ref[...], preferred_element_type=_f32)
    kk = k * kk_w_ref[...]
    kk = kk / jnp.maximum(jnp.sqrt(_head_sums(kk * kk)), 1e-12)
    r_ref[...] = r
    w_ref[...] = decay
    k_ref[...] = k * (1.0 + (a - 1.0) * ka_ref[...])
    v_ref[...] = v
    kk_ref[...] = kk
    b_ref[...] = kk * a
    g_ref[...] = g


def _rwkv_prep_prompt_kernel(pr_ref, pk_ref, pv_ref, pl_ref, tr_ref, tk_ref, tv_ref, tl_ref,
                             mu_ref, w0_ref, a0_ref, kk_w_ref, ka_ref, w2_ref, a2_ref, g2_ref, *outs):
    tt = pr_ref.shape[0]
    keep = jnp.where(lax.rem(pl.program_id(0), SEQ // tt) != 0, 1.0, 0.0)
    first = lax.broadcasted_iota(jnp.int32, (tt, 1), 0) == 0

    def shift(p_ref, t_ref, off, width):
        p = p_ref[...]
        last = t_ref[pl.ds(SUBLANES - 1, 1), :] * keep
        prev = jnp.where(first, last, pltpu.roll(p, 1, 0))
        return p + (prev - p) * mu_ref[:, off:off + width]

    _rwkv_prep_math(shift(pr_ref, tr_ref, 0, C_RWKV), shift(pk_ref, tk_ref, C_RWKV, C_RWKV),
                    shift(pv_ref, tv_ref, 2 * C_RWKV, C_RWKV), shift(pl_ref, tl_ref, 3 * C_RWKV, LORA_W),
                    w0_ref, a0_ref, kk_w_ref, ka_ref, w2_ref, a2_ref, g2_ref, outs)


def _rwkv_prep_sample_kernel(pr_ref, pk_ref, pv_ref, pl_ref, qr_ref, qk_ref, qv_ref, ql_ref,
                             mu_ref, w0_ref, a0_ref, kk_w_ref, ka_ref, w2_ref, a2_ref, g2_ref, *outs):
    def shift(p_ref, q_ref, off, width):
        p = p_ref[...]
        return p + (q_ref[...] - p) * mu_ref[:, off:off + width]

    _rwkv_prep_math(shift(pr_ref, qr_ref, 0, C_RWKV), shift(pk_ref, qk_ref, C_RWKV, C_RWKV),
                    shift(pv_ref, qv_ref, 2 * C_RWKV, C_RWKV), shift(pl_ref, ql_ref, 3 * C_RWKV, LORA_W),
                    w0_ref, a0_ref, kk_w_ref, ka_ref, w2_ref, a2_ref, g2_ref, outs)


def _rwkv_prep(h, shift_pad, params, prompt):
    tt = PREP_TILE if prompt else DEC_BATCH
    n = N_PROMPT if prompt else DEC_BATCH
    row0 = 0 if prompt else N_PROMPT // tt
    cb = P_R // C_RWKV
    main = lambda width, c: pl.BlockSpec((tt, width), lambda i: (row0 + i, c))
    const = lambda i: (0, 0)
    vec = pl.BlockSpec((1, C_RWKV), const)
    lw = pl.BlockSpec((LORA_W, C_RWKV), const)
    if prompt:
        per = tt // SUBLANES
        prev = lambda width, c: pl.BlockSpec((SUBLANES, width), lambda i: (jnp.maximum(i * per - 1, 0), c))
        prev_specs = [prev(C_RWKV, cb), prev(C_RWKV, cb + 1), prev(C_RWKV, cb + 2), prev(LORA_W, P_LORA // LORA_W)]
        prev_args = [h] * 4
        kern = _rwkv_prep_prompt_kernel
    else:
        prev = lambda width, c: pl.BlockSpec((tt, width), lambda i: (i, c))
        prev_specs = [prev(C_RWKV, 0), prev(C_RWKV, 1), prev(C_RWKV, 2), prev(LORA_W, 3 * C_RWKV // LORA_W)]
        prev_args = [shift_pad] * 4
        kern = _rwkv_prep_sample_kernel
    return pl.pallas_call(
        kern,
        grid=(n // tt,),
        in_specs=[main(C_RWKV, cb), main(C_RWKV, cb + 1), main(C_RWKV, cb + 2), main(LORA_W, P_LORA // LORA_W)]
        + prev_specs + [pl.BlockSpec((1, RWKV_PACK), const), vec, vec, vec, vec, lw, lw, lw],
        out_specs=[pl.BlockSpec((tt, C_RWKV), lambda i: (i, 0))] * 7,
        out_shape=[jax.ShapeDtypeStruct((n, C_RWKV), _f32)] * 7,
        compiler_params=_params(("parallel",), _VMEM_LIMIT),
        name="rwkv_prep_prompt" if prompt else "rwkv_prep_sample",
    )(h, h, h, h, *prev_args, *params)


def _wkv_prompt_kernel(w_ref, kk_ref, b_ref, k_ref, r_ref, v_ref, y_ref, z_ref):
    @pl.when(pl.program_id(0) == 0)
    def _():
        z_ref[...] = jnp.zeros_like(z_ref)

    half = RWKV_HEAD_SIZE // 2
    both = lambda x: jnp.concatenate([x, x], axis=-1)

    def step(t, carry):
        w = both(w_ref[t])
        kk = both(kk_ref[t])
        bb = both(b_ref[t])
        k = both(k_ref[t])
        r = both(r_ref[t])
        for il in range(half):
            z = z_ref[il]
            sa = -jnp.sum(z * kk, axis=0, keepdims=True)
            z = z * w + sa * bb + v_ref[t, pl.ds(il, 1), :] * k
            z_ref[il] = z
            y_ref[t, pl.ds(il, 1), :] = jnp.sum(z * r, axis=0, keepdims=True)
        return carry

    lax.fori_loop(0, w_ref.shape[0], step, 0)


def _wkv_prompt(w, kk, b, k, r, v):
    t = w.shape[0]
    tc = SCAN_CHUNK
    half = RWKV_HEAD_SIZE // 2
    key = pl.BlockSpec((tc, RWKV_HEAD_SIZE, BATCH * RWKV_HEADS), lambda i: (i, 0, 0))
    val = pl.BlockSpec((tc, half, LANES), lambda i: (i, 0, 0))
    return pl.pallas_call(
        _wkv_prompt_kernel,
        grid=(t // tc,),
        in_specs=[key] * 5 + [val],
        out_specs=[val, pl.BlockSpec((half, RWKV_HEAD_SIZE, LANES), lambda i: (0, 0, 0))],
        out_shape=[jax.ShapeDtypeStruct((t, half, LANES), _f32),
                   jax.ShapeDtypeStruct((half, RWKV_HEAD_SIZE, LANES), _f32)],
        compiler_params=_params(("arbitrary",), _VMEM_LIMIT),
        name="wkv_prompt",
    )(w, kk, b, k, r, v)


def _wkv_sample_kernel(s_ref, w_ref, kk_ref, b_ref, k_ref, r_ref, v_ref, y_ref, so_ref):
    s = s_ref[...]
    row = lambda ref: ref[...]
    eye = (lax.broadcasted_iota(jnp.int32, (RWKV_HEAD_SIZE, RWKV_HEAD_SIZE), 0)
           == lax.broadcasted_iota(jnp.int32, (RWKV_HEAD_SIZE, RWKV_HEAD_SIZE), 1))
    sa = -jnp.sum(s * row(kk_ref), axis=-1, keepdims=True)
    vcol = jnp.sum(jnp.where(eye, row(v_ref), 0.0), axis=-1, keepdims=True)
    s = s * row(w_ref) + sa * row(b_ref) + vcol * row(k_ref)
    so_ref[...] = s
    ycol = jnp.sum(s * row(r_ref), axis=-1, keepdims=True)
    y_ref[...] = jnp.sum(jnp.where(eye, ycol, 0.0), axis=-2, keepdims=True)


def _wkv_sample(s0, w, kk, b, k, r, v):
    tb = 8
    st = pl.BlockSpec((tb, RWKV_HEADS, RWKV_HEAD_SIZE, RWKV_HEAD_SIZE), lambda i: (i, 0, 0, 0))
    vec = pl.BlockSpec((tb, RWKV_HEADS, 1, RWKV_HEAD_SIZE), lambda i: (i, 0, 0, 0))
    return pl.pallas_call(
        _wkv_sample_kernel,
        grid=(DEC_BATCH // tb,),
        in_specs=[st] + [vec] * 6,
        out_specs=[vec, st],
        out_shape=[jax.ShapeDtypeStruct((DEC_BATCH, RWKV_HEADS, 1, RWKV_HEAD_SIZE), _f32),
                   jax.ShapeDtypeStruct(s0.shape, _f32)],
        compiler_params=_params(("parallel",)),
        name="wkv_sample",
    )(s0, w, kk, b, k, r, v)


def _rwkv_post_kernel(y_ref, r_ref, k_ref, v_ref, g_ref, rk_ref, lw_ref, lb_ref, o_ref):
    y = y_ref[...]
    inv = 1.0 / RWKV_HEAD_SIZE
    mean = _head_sums(y) * inv
    c = y - mean
    var = _head_sums(c * c) * inv
    yn = c * lax.rsqrt(var + GN_EPS) * lw_ref[...] + lb_ref[...]
    bonus = _head_sums(r_ref[...] * k_ref[...] * rk_ref[...]) * v_ref[...]
    o_ref[...] = ((yn + bonus) * g_ref[...]).astype(o_ref.dtype)


def _rwkv_post(y, r, k, v, g, r_k, ln_w, ln_b, tt):
    n = y.shape[0]
    big = pl.BlockSpec((tt, C_RWKV), lambda i: (i, 0))
    vec = pl.BlockSpec((1, C_RWKV), lambda i: (0, 0))
    return pl.pallas_call(
        _rwkv_post_kernel,
        grid=(n // tt,),
        in_specs=[big] * 5 + [vec] * 3,
        out_specs=big,
        out_shape=jax.ShapeDtypeStruct((n, C_RWKV), _bf16),
        compiler_params=_params(("parallel",), _VMEM_LIMIT),
        name="rwkv_post",
    )(y, r, k, v, g, r_k, ln_w, ln_b)


def _merge_kernel(ya_ref, yr_ref, yp_ref, wa_ref, wr_ref, wp_ref,
                  g0_ref, g1_ref, g2_ref, b0_ref, b1_ref, b2_ref, *rest):
    o_ref = rest[-1]
    acc = None
    for y_ref, w_ref, g_ref, b_ref in ((ya_ref, wa_ref, g0_ref, b0_ref),
                                       (yr_ref, wr_ref, g1_ref, b1_ref),
                                       (yp_ref, wp_ref, g2_ref, b2_ref)):
        term = jax.nn.sigmoid(g_ref[...] + b_ref[...]) * jnp.dot(
            y_ref[...], w_ref[...], preferred_element_type=_f32)
        acc = term if acc is None else acc + term
    o_ref[...] = acc.astype(o_ref.dtype)


def _merge(ya, yr, yp, wa, wr, wp, h, b_gate, into=None):
    prompt = into is None
    tm, tn = (PROMPT_TILE, 512) if prompt else (DEC_BATCH, 512)
    rows = N_PROMPT if prompt else DEC_BATCH
    row0 = 0 if prompt else N_PROMPT // tm
    nj = D_MODEL // tn
    ysp = pl.BlockSpec((tm, C_ATT), lambda i, j: (i, 0))
    wsp = pl.BlockSpec((C_ATT, tn), lambda i, j: (0, j))
    gsp = lambda br: pl.BlockSpec((tm, tn), lambda i, j: (row0 + i, br * nj + j))
    bsp = lambda br: pl.BlockSpec((1, tn), lambda i, j: (0, br * nj + j))
    in_specs = [ysp, ysp, ysp, wsp, wsp, wsp, gsp(0), gsp(1), gsp(2), bsp(0), bsp(1), bsp(2)]
    args = [ya, yr, yp, wa, wr, wp, h, h, h, b_gate, b_gate, b_gate]
    aliases = {}
    if not prompt:
        in_specs.append(pl.BlockSpec(memory_space=pl.ANY))
        args.append(into)
        aliases = {len(args) - 1: 0}
    return pl.pallas_call(
        _merge_kernel,
        grid=(rows // tm, nj),
        in_specs=in_specs,
        out_specs=pl.BlockSpec((tm, tn), lambda i, j: (row0 + i, j)),
        out_shape=jax.ShapeDtypeStruct((N_TOK, D_MODEL), _bf16),
        input_output_aliases=aliases,
        compiler_params=_params(("parallel", "parallel"), _VMEM_LIMIT),
        name="merge_prompt" if prompt else "merge_sample",
    )(*args)


_CAND_R2 = tuple(min(PEER_TOPK, PEER_TOPK // (r1 + 1)) for r1 in range(PEER_TOPK))
_CAND_ROW0 = tuple(sum(_CAND_R2[:r1]) for r1 in range(PEER_TOPK))
_N_CAND = sum(_CAND_R2)
_CAND_ROWS = -(-_N_CAND // SUBLANES) * SUBLANES


def _peer_route_kernel(q_ref, keys_ref, i_ref, j_ref, g_ref,
                       t_scr, ti_scr, cand_scr, cid_scr, bi_scr, bj_scr, bg_scr):
    tp = q_ref.shape[0]
    q = q_ref[...]
    key_iota = lax.broadcasted_iota(jnp.int32, (N_KEYS, tp), 0).astype(_f32)
    cand_iota = lax.broadcasted_iota(jnp.int32, (_CAND_ROWS, tp), 0).astype(_f32)
    neg_inf = -jnp.inf
    for hd in range(PEER_HEADS):
        for c in range(2):
            off = (hd * 2 + c) * PEER_HALF
            s = lax.dot_general(keys_ref[c], q[:, off:off + PEER_HALF],
                                (((1,), (1,)), ((), ())), preferred_element_type=_f32)
            for rk in range(PEER_TOPK):
                m = jnp.max(s, axis=0, keepdims=True)
                at_max = jnp.where(s == m, key_iota, float(N_KEYS))
                pos = jnp.min(at_max, axis=0, keepdims=True)
                t_scr[pl.ds(c * PEER_TOPK + rk, 1), :] = m
                ti_scr[pl.ds(c * PEER_TOPK + rk, 1), :] = pos
                s = jnp.where(at_max == pos, neg_inf, s)
        cand_scr[pl.ds(_CAND_ROWS - SUBLANES, SUBLANES), :] = jnp.full((SUBLANES, tp), neg_inf, _f32)
        cid_scr[pl.ds(_CAND_ROWS - SUBLANES, SUBLANES), :] = jnp.zeros((SUBLANES, tp), _f32)
        for r1 in range(PEER_TOPK):
            cnt = _CAND_R2[r1]
            cand_scr[pl.ds(_CAND_ROW0[r1], cnt), :] = t_scr[pl.ds(r1, 1), :] + t_scr[pl.ds(PEER_TOPK, cnt), :]
            cid_scr[pl.ds(_CAND_ROW0[r1], cnt), :] = (ti_scr[pl.ds(r1, 1), :] * float(N_KEYS)
                                                      + ti_scr[pl.ds(PEER_TOPK, cnt), :])
        cand = cand_scr[...]
        cid = cid_scr[...]
        base = hd * PEER_TOPK
        for rk in range(PEER_TOPK):
            m = jnp.max(cand, axis=0, keepdims=True)
            at_max = jnp.where(cand == m, cand_iota, float(_CAND_ROWS))
            pos = jnp.min(at_max, axis=0, keepdims=True)
            sel = at_max == pos
            eid = jnp.max(jnp.where(sel, cid, -1.0), axis=0, keepdims=True).astype(jnp.int32)
            bg_scr[pl.ds(base + rk, 1), :] = m
            bi_scr[pl.ds(base + rk, 1), :] = eid >> 7
            bj_scr[pl.ds(base + rk, 1), :] = eid & (N_KEYS - 1)
            cand = jnp.where(sel, neg_inf, cand)
        best = bg_scr[pl.ds(base, PEER_TOPK), :]
        e = jnp.exp(best - best[0:1, :])
        bg_scr[pl.ds(base, PEER_TOPK), :] = e / jnp.sum(e, axis=0, keepdims=True)
    i_ref[...] = bi_scr[...].T
    j_ref[...] = bj_scr[...].T
    g_ref[...] = bg_scr[...].T


def _peer_route(q, keys):
    tp = PEER_TILE
    n = q.shape[0]
    slots = PEER_HEADS * PEER_TOPK
    osp = pl.BlockSpec((tp, slots), lambda i: (i, 0))
    return pl.pallas_call(
        _peer_route_kernel,
        grid=(n // tp,),
        in_specs=[pl.BlockSpec((tp, q.shape[1]), lambda i: (i, 0)),
                  pl.BlockSpec((2, N_KEYS, PEER_HALF), lambda i: (0, 0, 0))],
        out_specs=[osp, osp, osp],
        out_shape=[jax.ShapeDtypeStruct((n, slots), jnp.int32),
                   jax.ShapeDtypeStruct((n, slots), jnp.int32),
                   jax.ShapeDtypeStruct((n, slots), _f32)],
        scratch_shapes=[pltpu.VMEM((2 * PEER_TOPK, tp), _f32),
                        pltpu.VMEM((2 * PEER_TOPK, tp), _f32),
                        pltpu.VMEM((_CAND_ROWS, tp), _f32),
                        pltpu.VMEM((_CAND_ROWS, tp), _f32),
                        pltpu.VMEM((slots, tp), jnp.int32),
                        pltpu.VMEM((slots, tp), jnp.int32),
                        pltpu.VMEM((slots, tp), _f32)],
        compiler_params=_params(("parallel",)),
        name="peer_route",
    )(q, keys)


def _peer_gates_kernel(i_ref, j_ref, g_ref, o_ref, tile_scr):
    tp = i_ref.shape[0]
    slots = i_ref.shape[1]
    key_iota = lax.broadcasted_iota(jnp.int32, (N_KEYS, slots), 0)

    def per_token(p, carry):
        irow = i_ref[pl.ds(p, 1), :]
        jrow = j_ref[pl.ds(p, 1), :]
        grow = g_ref[pl.ds(p, 1), :]
        at = jnp.where(key_iota == irow, grow, 0.0).astype(_bf16)
        bt = jnp.where(key_iota == jrow, 1.0, 0.0).astype(_bf16)
        tile = lax.dot_general(at, bt, (((1,), (1,)), ((), ())), preferred_element_type=_f32)
        tile_scr[pl.ds(pl.multiple_of(p * G_PITCH, SUBLANES), N_KEYS), :] = tile
        return carry

    lax.fori_loop(0, tp, per_token, 0, unroll=8)
    for i in range(N_KEYS):
        o_ref[:, i * N_KEYS:(i + 1) * N_KEYS] = tile_scr[pl.ds(i, tp, stride=G_PITCH), :].astype(o_ref.dtype)


def _peer_gates(isel, jsel, gate):
    tp = PEER_TILE
    n, slots = isel.shape
    isp = pl.BlockSpec((tp, slots), lambda i: (i, 0))
    return pl.pallas_call(
        _peer_gates_kernel,
        grid=(n // tp,),
        in_specs=[isp, isp, isp],
        out_specs=pl.BlockSpec((tp, N_EXPERTS), lambda i: (i, 0)),
        out_shape=jax.ShapeDtypeStruct((n, N_EXPERTS), _bf16),
        scratch_shapes=[pltpu.VMEM((tp * G_PITCH, N_KEYS), _f32)],
        compiler_params=_params(("parallel",), _VMEM_LIMIT),
        name="peer_gates",
    )(isel, jsel, gate)


def _peer_dense_kernel(x_ref, u_ref, v_ref, g_ref, res_ref, o_ref):
    @pl.when(pl.program_id(1) == 0)
    def _():
        o_ref[...] = res_ref[...]

    s = lax.dot_general(x_ref[...], u_ref[...], (((1,), (1,)), ((), ())), preferred_element_type=_f32)
    act = 0.5 * s * (1.0 + lax.erf(s * (2.0 ** -0.5)))
    wgt = (g_ref[...].astype(_f32) * act).astype(_bf16)
    o_ref[...] += jnp.dot(wgt, v_ref[...], preferred_element_type=_f32)


def _peer_dense(xn, u, v, gates, res):
    tm, te = TOK_TILE, EXPERT_TILE
    n, d = xn.shape
    return pl.pallas_call(
        _peer_dense_kernel,
        grid=(n // tm, N_EXPERTS // te),
        in_specs=[pl.BlockSpec((tm, d), lambda i, e: (i, 0)),
                  pl.BlockSpec((te, d), lambda i, e: (e, 0)),
                  pl.BlockSpec((te, d), lambda i, e: (e, 0)),
                  pl.BlockSpec((tm, te), lambda i, e: (i, e)),
                  pl.BlockSpec((tm, d), lambda i, e: (i, 0))],
        out_specs=pl.BlockSpec((tm, d), lambda i, e: (i, 0)),
        out_shape=jax.ShapeDtypeStruct((n, d), _f32),
        compiler_params=_params(("parallel", "arbitrary"), _VMEM_LIMIT),
        name="peer_dense",
    )(xn, u, v, gates, res)


def _pack_w_in(w):
    seg = lambda a, b: w[:, a:b]
    lora = seg(OFF_RWKV + 3 * C_RWKV, OFF_POOL)
    pad = lambda x, width: jnp.pad(x, ((0, 0), (0, width - x.shape[1])))
    packed = jnp.concatenate([
        seg(OFF_GATE, OFF_GATE + N_BRANCH * D_MODEL),
        seg(0, OFF_K),
        seg(OFF_POOL, OFF_GATE),
        seg(OFF_RWKV, OFF_RWKV + 3 * C_RWKV),
        pad(lora, LORA_W),
        seg(OFF_K, OFF_V),
        seg(OFF_V, OFF_RWKV),
    ], axis=1)
    return pad(packed, P_COLS).astype(_bf16)


def _pad_rows(w, rows, at):
    return jnp.zeros((rows, w.shape[1]), w.dtype).at[at:at + w.shape[0]].set(w).astype(_bf16)


def _to_key_tiles(x):
    t = x.reshape(BATCH, SEQ, RWKV_HEADS, RWKV_HEAD_SIZE).transpose(1, 3, 0, 2)
    return t.reshape(SEQ, RWKV_HEAD_SIZE, BATCH * RWKV_HEADS)


def _to_value_rows(x):
    half = RWKV_HEAD_SIZE // 2
    t = x.reshape(BATCH, SEQ, RWKV_HEADS, 2, half).transpose(1, 4, 3, 0, 2)
    return t.reshape(SEQ, half, 2 * BATCH * RWKV_HEADS)


def _from_value_rows(y):
    half = RWKV_HEAD_SIZE // 2
    t = y.reshape(SEQ, half, 2, BATCH, RWKV_HEADS).transpose(3, 0, 4, 2, 1)
    return t.reshape(N_PROMPT, C_RWKV)


def _state_from_lanes(z):
    half = RWKV_HEAD_SIZE // 2
    t = z.reshape(half, RWKV_HEAD_SIZE, 2, BATCH, RWKV_HEADS).transpose(3, 4, 2, 0, 1)
    return t.reshape(BATCH, RWKV_HEADS, RWKV_HEAD_SIZE, RWKV_HEAD_SIZE)


def _layer(x, lp, state):
    cache_k, cache_v, wkv0, shift0, pool0 = state
    row = lambda a: a.reshape(1, -1)

    xn = _rmsnorm(x, lp["norm_mix"], _bf16, TOK_TILE)
    h = _matmul(xn, lp["w_in"], TOK_TILE, 1024, _f32, "proj_in")
    hs = h[N_PROMPT:]
    tail = lambda rows, off, width: h[:N_PROMPT].reshape(BATCH, SEQ, P_COLS)[:, SEQ - rows:, off:off + width]

    ya_p = _attn_prompt(h, lp["sinks"])
    kn, vn = hs[:, P_KATT:P_KATT + C_KV], hs[:, P_VATT:P_VATT + C_KV]
    L = cache_k.shape[1]
    ya_s = _attn_sample(hs[:, P_Q:P_Q + C_ATT].reshape(DEC_BATCH, N_Q_HEADS, HEAD_DIM),
                        cache_k.reshape(DEC_BATCH, L, C_KV), cache_v.reshape(DEC_BATCH, L, C_KV),
                        kn, vn, lp["sinks"]).reshape(DEC_BATCH, C_ATT)
    kv_shape = (BATCH, WINDOW, N_KV_HEADS, HEAD_DIM)
    new_k_p = tail(WINDOW, P_KATT, C_KV).reshape(kv_shape)
    new_v_p = tail(WINDOW, P_VATT, C_KV).reshape(kv_shape)
    shape_kv = (DEC_BATCH, 1, N_KV_HEADS, HEAD_DIM)
    new_k_s = jnp.concatenate([cache_k, kn.reshape(shape_kv)], axis=1)[:, -L:]
    new_v_s = jnp.concatenate([cache_v, vn.reshape(shape_kv)], axis=1)[:, -L:]

    prep_params = (lp["mu"], row(lp["w0"]), row(lp["a0"]), row(lp["k_k"]), row(lp["k_a"]),
                   lp["w2"], lp["a2"], lp["g2"])
    r_p, w_p, k_p, v_p, kk_p, b_p, g_p = _rwkv_prep(h, None, prep_params, prompt=True)
    shift_pad = jnp.pad(shift0, ((0, 0), (0, RWKV_PACK - RWKV_COLS)))
    r_s, w_s, k_s, v_s, kk_s, b_s, g_s = _rwkv_prep(h, shift_pad, prep_params, prompt=False)
    y_lanes, z_lanes = _wkv_prompt(_to_key_tiles(w_p), _to_key_tiles(kk_p), _to_key_tiles(b_p),
                                   _to_key_tiles(k_p), _to_key_tiles(r_p), _to_value_rows(v_p))
    sm = lambda a: a.reshape(DEC_BATCH, RWKV_HEADS, 1, RWKV_HEAD_SIZE)
    y_s, new_wkv_s = _wkv_sample(wkv0, sm(w_s), sm(kk_s), sm(b_s), sm(k_s), sm(r_s), sm(v_s))
    post_params = (row(lp["r_k"]), row(lp["ln_w"]), row(lp["ln_b"]))
    yr_p = _rwkv_post(_from_value_rows(y_lanes), r_p, k_p, v_p, g_p, *post_params, tt=PREP_TILE)
    yr_s = _rwkv_post(y_s.reshape(DEC_BATCH, C_RWKV), r_s, k_s, v_s, g_s, *post_params, tt=DEC_BATCH)
    new_wkv_p = _state_from_lanes(z_lanes)
    new_shift_p = tail(1, P_R, RWKV_COLS).reshape(BATCH, RWKV_COLS)
    new_shift_s = hs[:, P_R:P_R + RWKV_COLS]

    yp_p = _pool_prompt(h, lp["pool_w"], row(lp["pool_scale"]))
    zs = hs[:, P_POOL:P_POOL + C_POOL]
    yp_s = _pool_sample(pool0.transpose(1, 0, 2), zs, lp["pool_w"], row(lp["pool_scale"]))
    new_pool_p = tail(POOL_BUF, P_POOL, C_POOL)
    new_pool_s = jnp.concatenate([pool0, zs[:, None]], axis=1)[:, -POOL_BUF:]

    mats = (lp["w_att_o"], lp["w_rwkv_o"], lp["w_pool_o"])
    merged = _merge(ya_p, yr_p, yp_p, *mats, h, row(lp["b_gate"]))
    merged = _merge(ya_s, yr_s, yp_s, *mats, h, row(lp["b_gate"]), into=merged)
    x = _matmul(merged, lp["w_out"], TOK_TILE, 1024, _f32, "proj_out", res=x)

    xn2 = _rmsnorm(x, lp["norm_ffn"], _bf16, TOK_TILE)
    q = _matmul(xn2, lp["peer_w_query"], TOK_TILE, 1024, _bf16, "peer_query")
    isel, jsel, gate = _peer_route(q, lp["peer_sub_keys"])
    gates = _peer_gates(isel, jsel, gate)
    x = _peer_dense(xn2, lp["peer_u"], lp["peer_v"], gates, x)

    st_p = (new_k_p, new_v_p, new_wkv_p, new_shift_p, new_pool_p)
    st_s = (new_k_s, new_v_s, new_wkv_s, new_shift_s, new_pool_s)
    return x, st_p, st_s


def kernel(x_prompt, x_sample, cache_k, cache_v, state_wkv, state_shift, state_pool, norm_mix, w_in, b_gate, attn_sinks, rwkv_mu, rwkv_w0, rwkv_w2, rwkv_a0, rwkv_a2, rwkv_g2, rwkv_k_k, rwkv_k_a, rwkv_r_k, rwkv_ln_w, rwkv_ln_b, pool_w, pool_scale, w_att_o, w_rwkv_o, w_pool_o, w_out, norm_ffn, peer_w_query, peer_sub_keys, peer_u, peer_v, norm_final):
    x = jnp.concatenate([x_prompt.reshape(N_PROMPT, D_MODEL), x_sample.reshape(DEC_BATCH, D_MODEL)], axis=0)
    new_p, new_s = [], []
    bf = lambda a: a.astype(_bf16)
    for l in range(DEPTH):
        mu_packed = jnp.pad(rwkv_mu[l], (0, RWKV_PACK - RWKV_COLS)).reshape(1, RWKV_PACK)
        lp = dict(
            norm_mix=norm_mix[l], w_in=_pack_w_in(w_in[l]), b_gate=b_gate[l], sinks=attn_sinks[l],
            mu=mu_packed, w0=rwkv_w0[l], a0=rwkv_a0[l], k_k=rwkv_k_k[l], k_a=rwkv_k_a[l],
            w2=_pad_rows(rwkv_w2[l], LORA_W, 0), a2=_pad_rows(rwkv_a2[l], LORA_W, W_LORA),
            g2=_pad_rows(rwkv_g2[l], LORA_W, W_LORA + A_LORA),
            r_k=rwkv_r_k[l], ln_w=rwkv_ln_w[l], ln_b=rwkv_ln_b[l],
            pool_w=bf(pool_w[l]), pool_scale=pool_scale[l],
            w_att_o=bf(w_att_o[l]), w_rwkv_o=bf(w_rwkv_o[l]), w_pool_o=bf(w_pool_o[l]), w_out=bf(w_out[l]),
            norm_ffn=norm_ffn[l], peer_w_query=bf(peer_w_query[l]), peer_sub_keys=bf(peer_sub_keys[l]),
            peer_u=bf(peer_u[l]), peer_v=bf(peer_v[l]))
        state = (cache_k[l], cache_v[l], state_wkv[l], state_shift[l], state_pool[l])
        x, st_p, st_s = _layer(x, lp, state)
        new_p.append(st_p)
        new_s.append(st_s)
    y = _rmsnorm(x, norm_final, _f32, TOK_TILE)
    y_prompt = y[:N_PROMPT].reshape(BATCH, SEQ, D_MODEL)
    y_sample = y[N_PROMPT:].reshape(DEC_BATCH, 1, D_MODEL)
    stack = lambda sts, i: jnp.stack([st[i] for st in sts])
    return (y_prompt, y_sample,
            stack(new_p, 0), stack(new_p, 1), stack(new_p, 2), stack(new_p, 3), stack(new_p, 4),
            stack(new_s, 0), stack(new_s, 1), stack(new_s, 2), stack(new_s, 3), stack(new_s, 4))
```

```python
import jax
import jax.numpy as jnp
from jax import lax
from jax.experimental import pallas as pl
from jax.experimental.pallas import tpu as pltpu

D_MODEL = 2048
BATCH = 4
SEQ = 2048
DEPTH = 2
DEC_BATCH = 128
PAST_LEN = 8192
HEAD_DIM = 64
N_Q_HEADS = 16
N_KV_HEADS = 2
Q_PER_KV = N_Q_HEADS // N_KV_HEADS
WINDOW = 128
C_ATT = N_Q_HEADS * HEAD_DIM
C_KV = N_KV_HEADS * HEAD_DIM
RWKV_HEADS = 16
RWKV_HEAD_SIZE = 64
C_RWKV = RWKV_HEADS * RWKV_HEAD_SIZE
W_LORA = 64
A_LORA = 64
G_LORA = 160
RWKV_COLS = 3 * C_RWKV + W_LORA + A_LORA + G_LORA
GN_EPS = 64e-5
POOL_WINDOWS = (2, 4, 8, 16)
POOL_GROUPS = len(POOL_WINDOWS)
C_POOL = 1024
POOL_GROUP_WIDTH = C_POOL // POOL_GROUPS
POOL_BUF = max(POOL_WINDOWS) - 1
N_BRANCH = 3
N_KEYS = 128
N_EXPERTS = N_KEYS * N_KEYS
PEER_HEADS = 8
PEER_HALF = 128
PEER_TOPK = 16
NORM_EPS = 1e-6
MASK_VALUE = -1e30

N_PROMPT = BATCH * SEQ
N_TOK = N_PROMPT + DEC_BATCH

OFF_K = C_ATT
OFF_V = OFF_K + C_KV
OFF_RWKV = OFF_V + C_KV
OFF_POOL = OFF_RWKV + RWKV_COLS
OFF_GATE = OFF_POOL + C_POOL

LORA_W = 512
P_GATE = 0
P_Q = P_GATE + N_BRANCH * D_MODEL
P_POOL = P_Q + C_ATT
P_R = P_POOL + C_POOL
P_K = P_R + C_RWKV
P_V = P_K + C_RWKV
P_LORA = P_V + C_RWKV
P_KATT = P_LORA + LORA_W
P_VATT = P_KATT + C_KV
P_COLS = 12288
RWKV_PACK = 3 * C_RWKV + LORA_W

SUBLANES = 8
LANES = 128
TOK_TILE = 832
PROMPT_TILE = 1024
MIX_TILE = 128
PREP_TILE = 256
PEER_TILE = 128
G_PITCH = 136
EXPERT_TILE = 1024
SCAN_CHUNK = 64

_VMEM_LIMIT = 56 * 1024 * 1024

_f32 = jnp.float32
_bf16 = jnp.bfloat16


def _params(sem, vmem=None):
    return pltpu.CompilerParams(dimension_semantics=sem, vmem_limit_bytes=vmem)


def _rmsnorm_kernel(x_ref, g_ref, o_ref):
    x = x_ref[...]
    y = x * lax.rsqrt(jnp.mean(x * x, axis=-1, keepdims=True) + NORM_EPS)
    o_ref[...] = (y * g_ref[...]).astype(o_ref.dtype)


def _rmsnorm(x, g, out_dtype, tile):
    n, d = x.shape
    return pl.pallas_call(
        _rmsnorm_kernel,
        grid=(n // tile,),
        in_specs=[pl.BlockSpec((tile, d), lambda i: (i, 0)),
                  pl.BlockSpec((1, d), lambda i: (0, 0))],
        out_specs=pl.BlockSpec((tile, d), lambda i: (i, 0)),
        out_shape=jax.ShapeDtypeStruct((n, d), out_dtype),
        compiler_params=_params(("parallel",), _VMEM_LIMIT),
        name="rmsnorm",
    )(x, g.reshape(1, d))


def _add_rmsnorm_kernel(x_ref, p_ref, g_ref, s_ref, o_ref):
    x = x_ref[...] + p_ref[...]
    s_ref[...] = x
    y = x * lax.rsqrt(jnp.mean(x * x, axis=-1, keepdims=True) + NORM_EPS)
    o_ref[...] = (y * g_ref[...]).astype(o_ref.dtype)


def _add_rmsnorm(x, p, g, out_dtype, tile):
    n, d = x.shape
    big = pl.BlockSpec((tile, d), lambda i: (i, 0))
    return pl.pallas_call(
        _add_rmsnorm_kernel,
        grid=(n // tile,),
        in_specs=[big, big, pl.BlockSpec((1, d), lambda i: (0, 0))],
        out_specs=[big, big],
        out_shape=[jax.ShapeDtypeStruct((n, d), _f32), jax.ShapeDtypeStruct((n, d), out_dtype)],
        compiler_params=_params(("parallel",), _VMEM_LIMIT),
        name="add_rmsnorm",
    )(x, p, g.reshape(1, d))


def _mm_kernel(a_ref, b_ref, o_ref):
    o_ref[...] = jnp.dot(a_ref[...], b_ref[...], preferred_element_type=_f32).astype(o_ref.dtype)


def _mm_res_kernel(a_ref, b_ref, r_ref, o_ref):
    o_ref[...] = r_ref[...] + jnp.dot(a_ref[...], b_ref[...], preferred_element_type=_f32)


def _matmul(a, b, tm, tn, out_dtype, name, res=None):
    m, k = a.shape
    n = b.shape[1]
    in_specs = [pl.BlockSpec((tm, k), lambda i, j: (i, 0)),
                pl.BlockSpec((k, tn), lambda i, j: (0, j))]
    args = [a, b]
    kern = _mm_kernel
    if res is not None:
        in_specs.append(pl.BlockSpec((tm, tn), lambda i, j: (i, j)))
        args.append(res)
        kern = _mm_res_kernel
    return pl.pallas_call(
        kern,
        grid=(m // tm, n // tn),
        in_specs=in_specs,
        out_specs=pl.BlockSpec((tm, tn), lambda i, j: (i, j)),
        out_shape=jax.ShapeDtypeStruct((m, n), out_dtype),
        compiler_params=_params(("parallel", "parallel"), _VMEM_LIMIT),
        name=name,
    )(*args)


def _attn_prompt_kernel(sink_ref, q_ref, kp_ref, kc_ref, vp_ref, vc_ref, o_ref):
    n = pl.program_id(1)
    q = q_ref[...].astype(_bf16)
    kband = jnp.concatenate([kp_ref[...], kc_ref[...]], axis=0).astype(_bf16)
    vband = jnp.concatenate([vp_ref[...], vc_ref[...]], axis=0).astype(_bf16)
    row = lax.broadcasted_iota(jnp.int32, (WINDOW, 2 * WINDOW), 0)
    col = lax.broadcasted_iota(jnp.int32, (WINDOW, 2 * WINDOW), 1)
    first_col = jnp.where(n > 0, 0, WINDOW)
    valid = (col <= row + WINDOW) & (col >= row) & (col >= first_col)
    outs = []
    for h in range(N_Q_HEADS):
        g = h // Q_PER_KV
        qh = q[:, h * HEAD_DIM:(h + 1) * HEAD_DIM]
        kg = kband[:, g * HEAD_DIM:(g + 1) * HEAD_DIM]
        vg = vband[:, g * HEAD_DIM:(g + 1) * HEAD_DIM]
        s = lax.dot_general(qh, kg, (((1,), (1,)), ((), ())), preferred_element_type=_f32)
        s = jnp.where(valid, s * (HEAD_DIM ** -0.5), MASK_VALUE)
        sink = sink_ref[h]
        m = jnp.maximum(jnp.max(s, axis=-1, keepdims=True), sink)
        p = jnp.exp(s - m)
        denom = jnp.sum(p, axis=-1, keepdims=True) + jnp.exp(sink - m)
        o = jnp.dot(p.astype(_bf16), vg, preferred_element_type=_f32)
        outs.append(o / denom)
    o_ref[...] = jnp.concatenate(outs, axis=-1).astype(o_ref.dtype)


def _attn_prompt(h, sinks):
    nb = SEQ // WINDOW
    qb, kb, vb = P_Q // C_ATT, P_KATT // C_KV, P_VATT // C_KV
    cur = lambda col: (lambda b, n: (b * nb + n, col))
    prev = lambda col: (lambda b, n: (b * nb + jnp.maximum(n - 1, 0), col))
    return pl.pallas_call(
        _attn_prompt_kernel,
        grid=(BATCH, nb),
        in_specs=[pl.BlockSpec(memory_space=pltpu.SMEM),
                  pl.BlockSpec((WINDOW, C_ATT), cur(qb)),
                  pl.BlockSpec((WINDOW, C_KV), prev(kb)),
                  pl.BlockSpec((WINDOW, C_KV), cur(kb)),
                  pl.BlockSpec((WINDOW, C_KV), prev(vb)),
                  pl.BlockSpec((WINDOW, C_KV), cur(vb))],
        out_specs=pl.BlockSpec((WINDOW, C_ATT), lambda b, n: (b * nb + n, 0)),
        out_shape=jax.ShapeDtypeStruct((N_PROMPT, C_ATT), _bf16),
        compiler_params=_params(("parallel", "parallel")),
        name="attn_prompt",
    )(sinks, h, h, h, h, h)


def _attn_sample_kernel(sink_ref, q_ref, kc_ref, vc_ref, kn_ref, vn_ref, o_ref):
    q = q_ref[...]
    kc = kc_ref[...].astype(_bf16)
    vc = vc_ref[...].astype(_bf16)
    kn = kn_ref[...].astype(_bf16).astype(_f32)
    vn = vn_ref[...].astype(_bf16).astype(_f32)
    hidx = lax.broadcasted_iota(jnp.int32, (1, Q_PER_KV, 1), 1)
    outs = []
    for g in range(N_KV_HEADS):
        sl = slice(g * HEAD_DIM, (g + 1) * HEAD_DIM)
        qg = q[:, g * Q_PER_KV:(g + 1) * Q_PER_KV, :].astype(_bf16)
        s = jnp.einsum("bhd,bld->bhl", qg, kc[:, :, sl], preferred_element_type=_f32)
        s_new = jnp.sum(qg.astype(_f32) * kn[:, None, sl], axis=-1, keepdims=True)
        scale = HEAD_DIM ** -0.5
        s = s * scale
        s_new = s_new * scale
        sink = jnp.zeros((1, Q_PER_KV, 1), _f32)
        for j in range(Q_PER_KV):
            sink = jnp.where(hidx == j, sink_ref[g * Q_PER_KV + j], sink)
        m = jnp.maximum(jnp.maximum(jnp.max(s, axis=-1, keepdims=True), s_new), sink)
        p = jnp.exp(s - m)
        p_new = jnp.exp(s_new - m)
        denom = jnp.sum(p, axis=-1, keepdims=True) + p_new + jnp.exp(sink - m)
        o = jnp.einsum("bhl,bld->bhd", p.astype(_bf16), vc[:, :, sl], preferred_element_type=_f32)
        o = o + p_new.astype(_bf16).astype(_f32) * vn[:, None, sl]
        outs.append(o / denom)
    o_ref[...] = jnp.concatenate(outs, axis=1).astype(o_ref.dtype)


def _attn_sample(q3, kc, vc, kn, vn, sinks):
    tb = 16
    L = kc.shape[1]
    return pl.pallas_call(
        _attn_sample_kernel,
        grid=(DEC_BATCH // tb,),
        in_specs=[pl.BlockSpec(memory_space=pltpu.SMEM),
                  pl.BlockSpec((tb, N_Q_HEADS, HEAD_DIM), lambda i: (i, 0, 0)),
                  pl.BlockSpec((tb, L, C_KV), lambda i: (i, 0, 0)),
                  pl.BlockSpec((tb, L, C_KV), lambda i: (i, 0, 0)),
                  pl.BlockSpec((tb, C_KV), lambda i: (i, 0)),
                  pl.BlockSpec((tb, C_KV), lambda i: (i, 0))],
        out_specs=pl.BlockSpec((tb, N_Q_HEADS, HEAD_DIM), lambda i: (i, 0, 0)),
        out_shape=jax.ShapeDtypeStruct((DEC_BATCH, N_Q_HEADS, HEAD_DIM), _bf16),
        compiler_params=_params(("parallel",)),
        name="attn_sample",
    )(sinks, q3, kc, vc, kn, vn)


def _pool_project(d_groups, pw_ref, scale_ref, o_ref):
    outs = []
    for gi in range(POOL_GROUPS):
        outs.append(jnp.dot(d_groups[gi].astype(_bf16), pw_ref[gi], preferred_element_type=_f32))
    o_ref[...] = (jnp.concatenate(outs, axis=-1) * scale_ref[...]).astype(o_ref.dtype)


def _pool_prompt_kernel(zp_ref, zc_ref, pw_ref, scale_ref, o_ref, ze_scr):
    n = pl.program_id(1)
    tt = zc_ref.shape[0]
    halo = 2 * SUBLANES
    prev_tail = zp_ref[pl.ds(tt - halo, halo), :]
    keep_rows = jnp.where(n > 0, halo, 0)
    ze_scr[pl.ds(0, halo), :] = jnp.where(
        lax.broadcasted_iota(jnp.int32, prev_tail.shape, 0) < keep_rows, prev_tail, 0.0)
    ze_scr[pl.ds(halo, tt), :] = zc_ref[...]
    pos = n * tt + lax.broadcasted_iota(jnp.int32, (tt, 1), 0)
    d_groups = []
    for gi, w in enumerate(POOL_WINDOWS):
        cs = slice(gi * POOL_GROUP_WIDTH, (gi + 1) * POOL_GROUP_WIDTH)
        z = ze_scr[pl.ds(halo, tt), cs]
        win = z
        for s in range(1, w):
            win = win + ze_scr[pl.ds(halo - s, tt), cs]
        cnt = jnp.minimum(w, pos + 1).astype(_f32)
        d_groups.append(win / cnt - z)
    _pool_project(d_groups, pw_ref, scale_ref, o_ref)


def _pool_prompt(h, pool_w, pool_scale):
    tt = MIX_TILE
    nt = SEQ // tt
    cb = P_POOL // C_POOL
    return pl.pallas_call(
        _pool_prompt_kernel,
        grid=(BATCH, nt),
        in_specs=[pl.BlockSpec((tt, C_POOL), lambda b, n: (b * nt + jnp.maximum(n - 1, 0), cb)),
                  pl.BlockSpec((tt, C_POOL), lambda b, n: (b * nt + n, cb)),
                  pl.BlockSpec((POOL_GROUPS, POOL_GROUP_WIDTH, POOL_GROUP_WIDTH), lambda b, n: (0, 0, 0)),
                  pl.BlockSpec((1, C_POOL), lambda b, n: (0, 0))],
        out_specs=pl.BlockSpec((tt, C_POOL), lambda b, n: (b * nt + n, 0)),
        out_shape=jax.ShapeDtypeStruct((N_PROMPT, C_POOL), _bf16),
        scratch_shapes=[pltpu.VMEM((tt + 2 * SUBLANES, C_POOL), _f32)],
        compiler_params=_params(("parallel", "parallel")),
        name="pool_prompt",
    )(h, h, pool_w, pool_scale)


def _pool_sample_kernel(past_ref, z_ref, pw_ref, scale_ref, o_ref):
    z = z_ref[...]
    d_groups = []
    for gi, w in enumerate(POOL_WINDOWS):
        cs = slice(gi * POOL_GROUP_WIDTH, (gi + 1) * POOL_GROUP_WIDTH)
        zg = z[:, cs]
        win = zg
        for s in range(1, w):
            win = win + past_ref[POOL_BUF - s][:, cs]
        d_groups.append(win / float(w) - zg)
    _pool_project(d_groups, pw_ref, scale_ref, o_ref)


def _pool_sample(past_t, z, pool_w, pool_scale):
    vm = pl.BlockSpec(memory_space=pltpu.VMEM)
    return pl.pallas_call(
        _pool_sample_kernel,
        in_specs=[vm, vm, vm, vm],
        out_specs=vm,
        out_shape=jax.ShapeDtypeStruct((DEC_BATCH, C_POOL), _bf16),
        name="pool_sample",
    )(past_t, z, pool_w, pool_scale)


def _head_sums(x):
    outs = []
    for hh in range(RWKV_HEADS):
        xs = x[:, hh * RWKV_HEAD_SIZE:(hh + 1) * RWKV_HEAD_SIZE]
        outs.append(jnp.broadcast_to(jnp.sum(xs, axis=-1, keepdims=True), xs.shape))
    return jnp.concatenate(outs, axis=-1)


def _rwkv_prep_math(r, k, v, lo, w0_ref, a0_ref, kk_w_ref, ka_ref, w2_ref, a2_ref, g2_ref, outs):
    r_ref, w_ref, k_ref, v_ref, kk_ref, b_ref, g_ref = outs
    lane = lax.broadcasted_iota(jnp.int32, lo.shape, 1)
    f = jnp.where(lane < W_LORA, jnp.tanh(lo),
                  jnp.where(lane < W_LORA + A_LORA, lo, jax.nn.sigmoid(lo))).astype(_bf16)
    wpre = w0_ref[...] + jnp.dot(f, w2_ref[...], preferred_element_type=_f32)
    neg = -wpre
    softplus = jnp.maximum(neg, 0.0) + jnp.log(1.0 + jnp.exp(-jnp.abs(neg)))
    w_log = -softplus - 0.5
    decay = jnp.exp(-jnp.exp(w_log))
    a = jax.nn.sigmoid(a0_ref[...] + jnp.dot(f, a2_ref[...], preferred_element_type=_f32))
    g = jnp.dot(f, g2_ref[...], preferred_element_type=_f32)
    kk = k * kk_w_ref[...]
    kk = kk / jnp.maximum(jnp.sqrt(_head_sums(kk * kk)), 1e-12)
    r_ref[...] = r
    w_ref[...] = decay
    k_ref[...] = k * (1.0 + (a - 1.0) * ka_ref[...])
    v_ref[...] = v
    kk_ref[...] = kk
    b_ref[...] = kk * a
    g_ref[...] = g


def _rwkv_prep_prompt_kernel(pr_ref, pk_ref, pv_ref, pl_ref, tr_ref, tk_ref, tv_ref, tl_ref,
                             mu_ref, w0_ref, a0_ref, kk_w_ref, ka_ref, w2_ref, a2_ref, g2_ref, *outs):
    tt = pr_ref.shape[0]
    keep = jnp.where(lax.rem(pl.program_id(0), SEQ // tt) != 0, 1.0, 0.0)
    first = lax.broadcasted_iota(jnp.int32, (tt, 1), 0) == 0

    def shift(p_ref, t_ref, off, width):
        p = p_ref[...]
        last = t_ref[pl.ds(SUBLANES - 1, 1), :] * keep
        prev = jnp.where(first, last, pltpu.roll(p, 1, 0))
        return p + (prev - p) * mu_ref[:, off:off + width]

    _rwkv_prep_math(shift(pr_ref, tr_ref, 0, C_RWKV), shift(pk_ref, tk_ref, C_RWKV, C_RWKV),
                    shift(pv_ref, tv_ref, 2 * C_RWKV, C_RWKV), shift(pl_ref, tl_ref, 3 * C_RWKV, LORA_W),
                    w0_ref, a0_ref, kk_w_ref, ka_ref, w2_ref, a2_ref, g2_ref, outs)


def _rwkv_prep_sample_kernel(pr_ref, pk_ref, pv_ref, pl_ref, qr_ref, qk_ref, qv_ref, ql_ref,
                             mu_ref, w0_ref, a0_ref, kk_w_ref, ka_ref, w2_ref, a2_ref, g2_ref, *outs):
    def shift(p_ref, q_ref, off, width):
        p = p_ref[...]
        return p + (q_ref[...] - p) * mu_ref[:, off:off + width]

    _rwkv_prep_math(shift(pr_ref, qr_ref, 0, C_RWKV), shift(pk_ref, qk_ref, C_RWKV, C_RWKV),
                    shift(pv_ref, qv_ref, 2 * C_RWKV, C_RWKV), shift(pl_ref, ql_ref, 3 * C_RWKV, LORA_W),
                    w0_ref, a0_ref, kk_w_ref, ka_ref, w2_ref, a2_ref, g2_ref, outs)


def _rwkv_prep(h, shift_pad, params, prompt):
    tt = PREP_TILE if prompt else DEC_BATCH
    n = N_PROMPT if prompt else DEC_BATCH
    row0 = 0 if prompt else N_PROMPT // tt
    cb = P_R // C_RWKV
    main = lambda width, c: pl.BlockSpec((tt, width), lambda i: (row0 + i, c))
    const = lambda i: (0, 0)
    vec = pl.BlockSpec((1, C_RWKV), const)
    lw = pl.BlockSpec((LORA_W, C_RWKV), const)
    if prompt:
        per = tt // SUBLANES
        prev = lambda width, c: pl.BlockSpec((SUBLANES, width), lambda i: (jnp.maximum(i * per - 1, 0), c))
        prev_specs = [prev(C_RWKV, cb), prev(C_RWKV, cb + 1), prev(C_RWKV, cb + 2), prev(LORA_W, P_LORA // LORA_W)]
        prev_args = [h] * 4
        kern = _rwkv_prep_prompt_kernel
    else:
        prev = lambda width, c: pl.BlockSpec((tt, width), lambda i: (i, c))
        prev_specs = [prev(C_RWKV, 0), prev(C_RWKV, 1), prev(C_RWKV, 2), prev(LORA_W, 3 * C_RWKV // LORA_W)]
        prev_args = [shift_pad] * 4
        kern = _rwkv_prep_sample_kernel
    return pl.pallas_call(
        kern,
        grid=(n // tt,),
        in_specs=[main(C_RWKV, cb), main(C_RWKV, cb + 1), main(C_RWKV, cb + 2), main(LORA_W, P_LORA // LORA_W)]
        + prev_specs + [pl.BlockSpec((1, RWKV_PACK), const), vec, vec, vec, vec, lw, lw, lw],
        out_specs=[pl.BlockSpec((tt, C_RWKV), lambda i: (i, 0))] * 7,
        out_shape=[jax.ShapeDtypeStruct((n, C_RWKV), _f32)] * 7,
        compiler_params=_params(("parallel",), _VMEM_LIMIT),
        name="rwkv_prep_prompt" if prompt else "rwkv_prep_sample",
    )(h, h, h, h, *prev_args, *params)


def _wkv_prompt_kernel(w_ref, kk_ref, b_ref, k_ref, r_ref, v_ref, y_ref, z_ref):
    @pl.when(pl.program_id(0) == 0)
    def _():
        z_ref[...] = jnp.zeros_like(z_ref)

    half = RWKV_HEAD_SIZE // 2

    def step(t, carry):
        w = w_ref[t]
        kk = kk_ref[t]
        bb = b_ref[t]
        k = k_ref[t]
        r = r_ref[t]
        for il in range(half):
            z = z_ref[il]
            sa = -jnp.sum(z * kk, axis=0, keepdims=True)
            z = z * w + sa * bb + v_ref[t, pl.ds(il, 1), :] * k
            z_ref[il] = z
            y_ref[t, pl.ds(il, 1), :] = jnp.sum(z * r, axis=0, keepdims=True)
        return carry

    lax.fori_loop(0, w_ref.shape[0], step, 0)


def _wkv_prompt(w, kk, b, k, r, v):
    t = w.shape[0]
    tc = SCAN_CHUNK
    half = RWKV_HEAD_SIZE // 2
    key = pl.BlockSpec((tc, RWKV_HEAD_SIZE, LANES), lambda i: (i, 0, 0))
    val = pl.BlockSpec((tc, half, LANES), lambda i: (i, 0, 0))
    return pl.pallas_call(
        _wkv_prompt_kernel,
        grid=(t // tc,),
        in_specs=[key] * 5 + [val],
        out_specs=[val, pl.BlockSpec((half, RWKV_HEAD_SIZE, LANES), lambda i: (0, 0, 0))],
        out_shape=[jax.ShapeDtypeStruct((t, half, LANES), _f32),
                   jax.ShapeDtypeStruct((half, RWKV_HEAD_SIZE, LANES), _f32)],
        compiler_params=_params(("arbitrary",), _VMEM_LIMIT),
        name="wkv_prompt",
    )(w, kk, b, k, r, v)


def _wkv_sample_kernel(s_ref, w_ref, kk_ref, b_ref, k_ref, r_ref, v_ref, y_ref, so_ref):
    s = s_ref[...]
    row = lambda ref: ref[...]
    eye = (lax.broadcasted_iota(jnp.int32, (RWKV_HEAD_SIZE, RWKV_HEAD_SIZE), 0)
           == lax.broadcasted_iota(jnp.int32, (RWKV_HEAD_SIZE, RWKV_HEAD_SIZE), 1))
    sa = -jnp.sum(s * row(kk_ref), axis=-1, keepdims=True)
    vcol = jnp.sum(jnp.where(eye, row(v_ref), 0.0), axis=-1, keepdims=True)
    s = s * row(w_ref) + sa * row(b_ref) + vcol * row(k_ref)
    so_ref[...] = s
    ycol = jnp.sum(s * row(r_ref), axis=-1, keepdims=True)
    y_ref[...] = jnp.sum(jnp.where(eye, ycol, 0.0), axis=-2, keepdims=True)


def _wkv_sample(s0, w, kk, b, k, r, v):
    tb = 8
    st = pl.BlockSpec((tb, RWKV_HEADS, RWKV_HEAD_SIZE, RWKV_HEAD_SIZE), lambda i: (i, 0, 0, 0))
    vec = pl.BlockSpec((tb, RWKV_HEADS, 1, RWKV_HEAD_SIZE), lambda i: (i, 0, 0, 0))
    return pl.pallas_call(
        _wkv_sample_kernel,
        grid=(DEC_BATCH // tb,),
        in_specs=[st] + [vec] * 6,
        out_specs=[vec, st],
        out_shape=[jax.ShapeDtypeStruct((DEC_BATCH, RWKV_HEADS, 1, RWKV_HEAD_SIZE), _f32),
                   jax.ShapeDtypeStruct(s0.shape, _f32)],
        compiler_params=_params(("parallel",)),
        name="wkv_sample",
    )(s0, w, kk, b, k, r, v)


def _rwkv_post_kernel(y_ref, r_ref, k_ref, v_ref, g_ref, rk_ref, lw_ref, lb_ref, o_ref):
    y = y_ref[...]
    inv = 1.0 / RWKV_HEAD_SIZE
    mean = _head_sums(y) * inv
    c = y - mean
    var = _head_sums(c * c) * inv
    yn = c * lax.rsqrt(var + GN_EPS) * lw_ref[...] + lb_ref[...]
    bonus = _head_sums(r_ref[...] * k_ref[...] * rk_ref[...]) * v_ref[...]
    o_ref[...] = ((yn + bonus) * g_ref[...]).astype(o_ref.dtype)


def _rwkv_post(y, r, k, v, g, r_k, ln_w, ln_b, tt):
    n = y.shape[0]
    big = pl.BlockSpec((tt, C_RWKV), lambda i: (i, 0))
    vec = pl.BlockSpec((1, C_RWKV), lambda i: (0, 0))
    return pl.pallas_call(
        _rwkv_post_kernel,
        grid=(n // tt,),
        in_specs=[big] * 5 + [vec] * 3,
        out_specs=big,
        out_shape=jax.ShapeDtypeStruct((n, C_RWKV), _bf16),
        compiler_params=_params(("parallel",), _VMEM_LIMIT),
        name="rwkv_post",
    )(y, r, k, v, g, r_k, ln_w, ln_b)


def _merge_kernel(ya_ref, yr_ref, yp_ref, wa_ref, wr_ref, wp_ref,
                  g0_ref, g1_ref, g2_ref, b0_ref, b1_ref, b2_ref, *rest):
    o_ref = rest[-1]
    acc = None
    for y_ref, w_ref, g_ref, b_ref in ((ya_ref, wa_ref, g0_ref, b0_ref),
                                       (yr_ref, wr_ref, g1_ref, b1_ref),
                                       (yp_ref, wp_ref, g2_ref, b2_ref)):
        term = jax.nn.sigmoid(g_ref[...] + b_ref[...]) * jnp.dot(
            y_ref[...], w_ref[...], preferred_element_type=_f32)
        acc = term if acc is None else acc + term
    o_ref[...] = acc.astype(o_ref.dtype)


def _merge(ya, yr, yp, wa, wr, wp, h, b_gate, into=None):
    prompt = into is None
    tm, tn = (PROMPT_TILE, 512) if prompt else (DEC_BATCH, 512)
    rows = N_PROMPT if prompt else DEC_BATCH
    row0 = 0 if prompt else N_PROMPT // tm
    nj = D_MODEL // tn
    ysp = pl.BlockSpec((tm, C_ATT), lambda i, j: (i, 0))
    wsp = pl.BlockSpec((C_ATT, tn), lambda i, j: (0, j))
    gsp = lambda br: pl.BlockSpec((tm, tn), lambda i, j: (row0 + i, br * nj + j))
    bsp = lambda br: pl.BlockSpec((1, tn), lambda i, j: (0, br * nj + j))
    in_specs = [ysp, ysp, ysp, wsp, wsp, wsp, gsp(0), gsp(1), gsp(2), bsp(0), bsp(1), bsp(2)]
    args = [ya, yr, yp, wa, wr, wp, h, h, h, b_gate, b_gate, b_gate]
    aliases = {}
    if not prompt:
        in_specs.append(pl.BlockSpec(memory_space=pl.ANY))
        args.append(into)
        aliases = {len(args) - 1: 0}
    return pl.pallas_call(
        _merge_kernel,
        grid=(rows // tm, nj),
        in_specs=in_specs,
        out_specs=pl.BlockSpec((tm, tn), lambda i, j: (row0 + i, j)),
        out_shape=jax.ShapeDtypeStruct((N_TOK, D_MODEL), _bf16),
        input_output_aliases=aliases,
        compiler_params=_params(("parallel", "parallel"), _VMEM_LIMIT),
        name="merge_prompt" if prompt else "merge_sample",
    )(*args)


_CAND_R2 = tuple(min(PEER_TOPK, PEER_TOPK // (r1 + 1)) for r1 in range(PEER_TOPK))
_CAND_ROW0 = tuple(sum(_CAND_R2[:r1]) for r1 in range(PEER_TOPK))
_N_CAND = sum(_CAND_R2)
_CAND_ROWS = -(-_N_CAND // SUBLANES) * SUBLANES


def _peer_route_kernel(q_ref, keys_ref, i_ref, j_ref, g_ref,
                       t_scr, ti_scr, cand_scr, cid_scr, bi_scr, bj_scr, bg_scr):
    tp = q_ref.shape[0]
    q = q_ref[...]
    key_iota = lax.broadcasted_iota(jnp.int32, (N_KEYS, tp), 0).astype(_f32)
    cand_iota = lax.broadcasted_iota(jnp.int32, (_CAND_ROWS, tp), 0).astype(_f32)
    neg_inf = -jnp.inf
    for hd in range(PEER_HEADS):
        for c in range(2):
            off = (hd * 2 + c) * PEER_HALF
            s = lax.dot_general(keys_ref[c], q[:, off:off + PEER_HALF],
                                (((1,), (1,)), ((), ())), preferred_element_type=_f32)
            for rk in range(PEER_TOPK):
                m = jnp.max(s, axis=0, keepdims=True)
                at_max = jnp.where(s == m, key_iota, float(N_KEYS))
                pos = jnp.min(at_max, axis=0, keepdims=True)
                t_scr[pl.ds(c * PEER_TOPK + rk, 1), :] = m
                ti_scr[pl.ds(c * PEER_TOPK + rk, 1), :] = pos
                s = jnp.where(at_max == pos, neg_inf, s)
        cand_scr[pl.ds(_CAND_ROWS - SUBLANES, SUBLANES), :] = jnp.full((SUBLANES, tp), neg_inf, _f32)
        cid_scr[pl.ds(_CAND_ROWS - SUBLANES, SUBLANES), :] = jnp.zeros((SUBLANES, tp), _f32)
        for r1 in range(PEER_TOPK):
            cnt = _CAND_R2[r1]
            cand_scr[pl.ds(_CAND_ROW0[r1], cnt), :] = t_scr[pl.ds(r1, 1), :] + t_scr[pl.ds(PEER_TOPK, cnt), :]
            cid_scr[pl.ds(_CAND_ROW0[r1], cnt), :] = (ti_scr[pl.ds(r1, 1), :] * float(N_KEYS)
                                                      + ti_scr[pl.ds(PEER_TOPK, cnt), :])
        cand = cand_scr[...]
        cid = cid_scr[...]
        base = hd * PEER_TOPK
        for rk in range(PEER_TOPK):
            m = jnp.max(cand, axis=0, keepdims=True)
            at_max = jnp.where(cand == m, cand_iota, float(_CAND_ROWS))
            pos = jnp.min(at_max, axis=0, keepdims=True)
            sel = at_max == pos
            eid = jnp.max(jnp.where(sel, cid, -1.0), axis=0, keepdims=True).astype(jnp.int32)
            bg_scr[pl.ds(base + rk, 1), :] = m
            bi_scr[pl.ds(base + rk, 1), :] = eid >> 7
            bj_scr[pl.ds(base + rk, 1), :] = eid & (N_KEYS - 1)
            cand = jnp.where(sel, neg_inf, cand)
        best = bg_scr[pl.ds(base, PEER_TOPK), :]
        e = jnp.exp(best - best[0:1, :])
        bg_scr[pl.ds(base, PEER_TOPK), :] = e / jnp.sum(e, axis=0, keepdims=True)
    i_ref[...] = bi_scr[...].T
    j_ref[...] = bj_scr[...].T
    g_ref[...] = bg_scr[...].T


def _peer_route(q, keys):
    tp = PEER_TILE
    n = q.shape[0]
    slots = PEER_HEADS * PEER_TOPK
    osp = pl.BlockSpec((tp, slots), lambda i: (i, 0))
    return pl.pallas_call(
        _peer_route_kernel,
        grid=(n // tp,),
        in_specs=[pl.BlockSpec((tp, q.shape[1]), lambda i: (i, 0)),
                  pl.BlockSpec((2, N_KEYS, PEER_HALF), lambda i: (0, 0, 0))],
        out_specs=[osp, osp, osp],
        out_shape=[jax.ShapeDtypeStruct((n, slots), jnp.int32),
                   jax.ShapeDtypeStruct((n, slots), jnp.int32),
                   jax.ShapeDtypeStruct((n, slots), _f32)],
        scratch_shapes=[pltpu.VMEM((2 * PEER_TOPK, tp), _f32),
                        pltpu.VMEM((2 * PEER_TOPK, tp), _f32),
                        pltpu.VMEM((_CAND_ROWS, tp), _f32),
                        pltpu.VMEM((_CAND_ROWS, tp), _f32),
                        pltpu.VMEM((slots, tp), jnp.int32),
                        pltpu.VMEM((slots, tp), jnp.int32),
                        pltpu.VMEM((slots, tp), _f32)],
        compiler_params=_params(("parallel",)),
        name="peer_route",
    )(q, keys)


def _peer_gates_kernel(i_ref, j_ref, g_ref, o_ref, tile_scr):
    tp = i_ref.shape[0]
    slots = i_ref.shape[1]
    key_iota = lax.broadcasted_iota(jnp.int32, (N_KEYS, slots), 0)

    def per_token(p, carry):
        irow = i_ref[pl.ds(p, 1), :]
        jrow = j_ref[pl.ds(p, 1), :]
        grow = g_ref[pl.ds(p, 1), :]
        at = jnp.where(key_iota == irow, grow, 0.0).astype(_bf16)
        bt = jnp.where(key_iota == jrow, 1.0, 0.0).astype(_bf16)
        tile = lax.dot_general(at, bt, (((1,), (1,)), ((), ())), preferred_element_type=_f32)
        tile_scr[pl.ds(pl.multiple_of(p * G_PITCH, SUBLANES), N_KEYS), :] = tile
        return carry

    lax.fori_loop(0, tp, per_token, 0, unroll=8)
    for i in range(N_KEYS):
        o_ref[:, i * N_KEYS:(i + 1) * N_KEYS] = tile_scr[pl.ds(i, tp, stride=G_PITCH), :].astype(o_ref.dtype)


def _peer_gates(isel, jsel, gate):
    tp = PEER_TILE
    n, slots = isel.shape
    isp = pl.BlockSpec((tp, slots), lambda i: (i, 0))
    return pl.pallas_call(
        _peer_gates_kernel,
        grid=(n // tp,),
        in_specs=[isp, isp, isp],
        out_specs=pl.BlockSpec((tp, N_EXPERTS), lambda i: (i, 0)),
        out_shape=jax.ShapeDtypeStruct((n, N_EXPERTS), _bf16),
        scratch_shapes=[pltpu.VMEM((tp * G_PITCH, N_KEYS), _f32)],
        compiler_params=_params(("parallel",), _VMEM_LIMIT),
        name="peer_gates",
    )(isel, jsel, gate)


def _peer_dense_kernel(x_ref, u_ref, v_ref, g_ref, o_ref):
    @pl.when(pl.program_id(1) == 0)
    def _():
        o_ref[...] = jnp.zeros_like(o_ref)

    s = lax.dot_general(x_ref[...], u_ref[...], (((1,), (1,)), ((), ())), preferred_element_type=_f32)
    act = 0.5 * s * (1.0 + lax.erf(s * (2.0 ** -0.5)))
    wgt = (g_ref[...].astype(_f32) * act).astype(_bf16)
    o_ref[...] += jnp.dot(wgt, v_ref[...], preferred_element_type=_f32)


def _peer_dense(xn, u, v, gates):
    tm, te = TOK_TILE, EXPERT_TILE
    n, d = xn.shape
    return pl.pallas_call(
        _peer_dense_kernel,
        grid=(n // tm, N_EXPERTS // te),
        in_specs=[pl.BlockSpec((tm, d), lambda i, e: (i, 0)),
                  pl.BlockSpec((te, d), lambda i, e: (e, 0)),
                  pl.BlockSpec((te, d), lambda i, e: (e, 0)),
                  pl.BlockSpec((tm, te), lambda i, e: (i, e))],
        out_specs=pl.BlockSpec((tm, d), lambda i, e: (i, 0)),
        out_shape=jax.ShapeDtypeStruct((n, d), _f32),
        compiler_params=_params(("parallel", "arbitrary"), _VMEM_LIMIT),
        name="peer_dense",
    )(xn, u, v, gates)


def _pack_w_in(w):
    seg = lambda a, b: w[:, a:b]
    lora = seg(OFF_RWKV + 3 * C_RWKV, OFF_POOL)
    pad = lambda x, width: jnp.pad(x, ((0, 0), (0, width - x.shape[1])))
    packed = jnp.concatenate([
        seg(OFF_GATE, OFF_GATE + N_BRANCH * D_MODEL),
        seg(0, OFF_K),
        seg(OFF_POOL, OFF_GATE),
        seg(OFF_RWKV, OFF_RWKV + 3 * C_RWKV),
        pad(lora, LORA_W),
        seg(OFF_K, OFF_V),
        seg(OFF_V, OFF_RWKV),
    ], axis=1)
    return pad(packed, P_COLS).astype(_bf16)


def _pad_rows(w, rows, at):
    return jnp.zeros((rows, w.shape[1]), w.dtype).at[at:at + w.shape[0]].set(w).astype(_bf16)


def _to_key_tiles(x):
    t = x.reshape(1, BATCH, SEQ, RWKV_HEADS, RWKV_HEAD_SIZE)
    t = jnp.broadcast_to(t, (2,) + t.shape[1:]).transpose(2, 4, 0, 1, 3)
    return t.reshape(SEQ, RWKV_HEAD_SIZE, 2 * BATCH * RWKV_HEADS)


def _to_value_rows(x):
    half = RWKV_HEAD_SIZE // 2
    t = x.reshape(BATCH, SEQ, RWKV_HEADS, 2, half).transpose(1, 4, 3, 0, 2)
    return t.reshape(SEQ, half, 2 * BATCH * RWKV_HEADS)


def _from_value_rows(y):
    half = RWKV_HEAD_SIZE // 2
    t = y.reshape(SEQ, half, 2, BATCH, RWKV_HEADS).transpose(3, 0, 4, 2, 1)
    return t.reshape(N_PROMPT, C_RWKV)


def _state_from_lanes(z):
    half = RWKV_HEAD_SIZE // 2
    t = z.reshape(half, RWKV_HEAD_SIZE, 2, BATCH, RWKV_HEADS).transpose(3, 4, 2, 0, 1)
    return t.reshape(BATCH, RWKV_HEADS, RWKV_HEAD_SIZE, RWKV_HEAD_SIZE)


def _layer(x, peer_prev, lp, state):
    cache_k, cache_v, wkv0, shift0, pool0 = state
    row = lambda a: a.reshape(1, -1)

    if peer_prev is None:
        xn = _rmsnorm(x, lp["norm_mix"], _bf16, TOK_TILE)
    else:
        x, xn = _add_rmsnorm(x, peer_prev, lp["norm_mix"], _bf16, TOK_TILE // 2)
    h = _matmul(xn, lp["w_in"], TOK_TILE, 1024, _f32, "proj_in")
    hs = h[N_PROMPT:]
    tail = lambda rows, off, width: jnp.stack(
        [lax.slice(h, ((b + 1) * SEQ - rows, off), ((b + 1) * SEQ, off + width)) for b in range(BATCH)])

    ya_p = _attn_prompt(h, lp["sinks"])
    kn, vn = hs[:, P_KATT:P_KATT + C_KV], hs[:, P_VATT:P_VATT + C_KV]
    L = cache_k.shape[1]
    ya_s = _attn_sample(hs[:, P_Q:P_Q + C_ATT].reshape(DEC_BATCH, N_Q_HEADS, HEAD_DIM),
                        cache_k.reshape(DEC_BATCH, L, C_KV), cache_v.reshape(DEC_BATCH, L, C_KV),
                        kn, vn, lp["sinks"]).reshape(DEC_BATCH, C_ATT)
    kv_shape = (BATCH, WINDOW, N_KV_HEADS, HEAD_DIM)
    new_k_p = tail(WINDOW, P_KATT, C_KV).reshape(kv_shape)
    new_v_p = tail(WINDOW, P_VATT, C_KV).reshape(kv_shape)
    shape_kv = (DEC_BATCH, 1, N_KV_HEADS, HEAD_DIM)
    new_k_s = jnp.concatenate([cache_k, kn.reshape(shape_kv)], axis=1)[:, -L:]
    new_v_s = jnp.concatenate([cache_v, vn.reshape(shape_kv)], axis=1)[:, -L:]

    prep_params = (lp["mu"], row(lp["w0"]), row(lp["a0"]), row(lp["k_k"]), row(lp["k_a"]),
                   lp["w2"], lp["a2"], lp["g2"])
    r_p, w_p, k_p, v_p, kk_p, b_p, g_p = _rwkv_prep(h, None, prep_params, prompt=True)
    shift_pad = jnp.pad(shift0, ((0, 0), (0, RWKV_PACK - RWKV_COLS)))
    r_s, w_s, k_s, v_s, kk_s, b_s, g_s = _rwkv_prep(h, shift_pad, prep_params, prompt=False)
    y_lanes, z_lanes = _wkv_prompt(_to_key_tiles(w_p), _to_key_tiles(kk_p), _to_key_tiles(b_p),
                                   _to_key_tiles(k_p), _to_key_tiles(r_p), _to_value_rows(v_p))
    sm = lambda a: a.reshape(DEC_BATCH, RWKV_HEADS, 1, RWKV_HEAD_SIZE)
    y_s, new_wkv_s = _wkv_sample(wkv0, sm(w_s), sm(kk_s), sm(b_s), sm(k_s), sm(r_s), sm(v_s))
    post_params = (row(lp["r_k"]), row(lp["ln_w"]), row(lp["ln_b"]))
    yr_p = _rwkv_post(_from_value_rows(y_lanes), r_p, k_p, v_p, g_p, *post_params, tt=PREP_TILE)
    yr_s = _rwkv_post(y_s.reshape(DEC_BATCH, C_RWKV), r_s, k_s, v_s, g_s, *post_params, tt=DEC_BATCH)
    new_wkv_p = _state_from_lanes(z_lanes)
    new_shift_p = tail(1, P_R, RWKV_COLS).reshape(BATCH, RWKV_COLS)
    new_shift_s = hs[:, P_R:P_R + RWKV_COLS]

    yp_p = _pool_prompt(h, lp["pool_w"], row(lp["pool_scale"]))
    zs = hs[:, P_POOL:P_POOL + C_POOL]
    yp_s = _pool_sample(pool0.transpose(1, 0, 2), zs, lp["pool_w"], row(lp["pool_scale"]))
    new_pool_p = tail(POOL_BUF, P_POOL, C_POOL)
    new_pool_s = jnp.concatenate([pool0, zs[:, None]], axis=1)[:, -POOL_BUF:]

    mats = (lp["w_att_o"], lp["w_rwkv_o"], lp["w_pool_o"])
    merged = _merge(ya_p, yr_p, yp_p, *mats, h, row(lp["b_gate"]))
    merged = _merge(ya_s, yr_s, yp_s, *mats, h, row(lp["b_gate"]), into=merged)
    x = _matmul(merged, lp["w_out"], TOK_TILE, 1024, _f32, "proj_out", res=x)

    xn2 = _rmsnorm(x, lp["norm_ffn"], _bf16, TOK_TILE)
    q = _matmul(xn2, lp["peer_w_query"], TOK_TILE, 1024, _bf16, "peer_query")
    isel, jsel, gate = _peer_route(q, lp["peer_sub_keys"])
    gates = _peer_gates(isel, jsel, gate)
    peer = _peer_dense(xn2, lp["peer_u"], lp["peer_v"], gates)

    st_p = (new_k_p, new_v_p, new_wkv_p, new_shift_p, new_pool_p)
    st_s = (new_k_s, new_v_s, new_wkv_s, new_shift_s, new_pool_s)
    return x, peer, st_p, st_s


def kernel(x_prompt, x_sample, cache_k, cache_v, state_wkv, state_shift, state_pool, norm_mix, w_in, b_gate, attn_sinks, rwkv_mu, rwkv_w0, rwkv_w2, rwkv_a0, rwkv_a2, rwkv_g2, rwkv_k_k, rwkv_k_a, rwkv_r_k, rwkv_ln_w, rwkv_ln_b, pool_w, pool_scale, w_att_o, w_rwkv_o, w_pool_o, w_out, norm_ffn, peer_w_query, peer_sub_keys, peer_u, peer_v, norm_final):
    x = jnp.concatenate([x_prompt.reshape(N_PROMPT, D_MODEL), x_sample.reshape(DEC_BATCH, D_MODEL)], axis=0)
    new_p, new_s = [], []
    peer = None
    bf = lambda a: a.astype(_bf16)
    for l in range(DEPTH):
        mu_packed = jnp.pad(rwkv_mu[l], (0, RWKV_PACK - RWKV_COLS)).reshape(1, RWKV_PACK)
        lp = dict(
            norm_mix=norm_mix[l], w_in=_pack_w_in(w_in[l]), b_gate=b_gate[l], sinks=attn_sinks[l],
            mu=mu_packed, w0=rwkv_w0[l], a0=rwkv_a0[l], k_k=rwkv_k_k[l], k_a=rwkv_k_a[l],
            w2=_pad_rows(rwkv_w2[l], LORA_W, 0), a2=_pad_rows(rwkv_a2[l], LORA_W, W_LORA),
            g2=_pad_rows(rwkv_g2[l], LORA_W, W_LORA + A_LORA),
            r_k=rwkv_r_k[l], ln_w=rwkv_ln_w[l], ln_b=rwkv_ln_b[l],
            pool_w=bf(pool_w[l]), pool_scale=pool_scale[l],
            w_att_o=bf(w_att_o[l]), w_rwkv_o=bf(w_rwkv_o[l]), w_pool_o=bf(w_pool_o[l]), w_out=bf(w_out[l]),
            norm_ffn=norm_ffn[l], peer_w_query=bf(peer_w_query[l]), peer_sub_keys=bf(peer_sub_keys[l]),
            peer_u=bf(peer_u[l]), peer_v=bf(peer_v[l]))
        state = (cache_k[l], cache_v[l], state_wkv[l], state_shift[l], state_pool[l])
        x, peer, st_p, st_s = _layer(x, peer, lp, state)
        new_p.append(st_p)
        new_s.append(st_s)
    _, y = _add_rmsnorm(x, peer, norm_final, _f32, TOK_TILE // 2)
    y_prompt = y[:N_PROMPT].reshape(BATCH, SEQ, D_MODEL)
    y_sample = y[N_PROMPT:].reshape(DEC_BATCH, 1, D_MODEL)
    stack = lambda sts, i: jnp.stack([st[i] for st in sts])
    return (y_prompt, y_sample,
            stack(new_p, 0), stack(new_p, 1), stack(new_p, 2), stack(new_p, 3), stack(new_p, 4),
            stack(new_s, 0), stack(new_s, 1), stack(new_s, 2), stack(new_s, 3), stack(new_s, 4))
```

```python
import functools

import jax
import jax.numpy as jnp
from jax import lax
from jax.experimental import pallas as pl
from jax.experimental.pallas import tpu as pltpu

D_MODEL = 2048
BATCH = 4
SEQ = 2048
DEPTH = 2
DEC_BATCH = 128
PAST_LEN = 8192
HEAD_DIM = 64
N_Q_HEADS = 16
N_KV_HEADS = 2
Q_PER_KV = N_Q_HEADS // N_KV_HEADS
WINDOW = 128
C_ATT = N_Q_HEADS * HEAD_DIM
C_KV = N_KV_HEADS * HEAD_DIM
RWKV_HEADS = 16
RWKV_HEAD_SIZE = 64
C_RWKV = RWKV_HEADS * RWKV_HEAD_SIZE
W_LORA = 64
A_LORA = 64
G_LORA = 160
RWKV_COLS = 3 * C_RWKV + W_LORA + A_LORA + G_LORA
GN_EPS = 64e-5
POOL_WINDOWS = (2, 4, 8, 16)
POOL_GROUPS = len(POOL_WINDOWS)
C_POOL = 1024
POOL_GROUP_WIDTH = C_POOL // POOL_GROUPS
POOL_BUF = max(POOL_WINDOWS) - 1
N_BRANCH = 3
N_KEYS = 128
N_EXPERTS = N_KEYS * N_KEYS
PEER_HEADS = 8
PEER_HALF = 128
PEER_TOPK = 16
NORM_EPS = 1e-6
MASK_VALUE = -1e30

N_PROMPT = BATCH * SEQ
N_TOK = N_PROMPT + DEC_BATCH

OFF_K = C_ATT
OFF_V = OFF_K + C_KV
OFF_RWKV = OFF_V + C_KV
OFF_POOL = OFF_RWKV + RWKV_COLS
OFF_GATE = OFF_POOL + C_POOL

LORA_W = 512
P_GATE = 0
P_Q = P_GATE + N_BRANCH * D_MODEL
P_POOL = P_Q + C_ATT
P_R = P_POOL + C_POOL
P_K = P_R + C_RWKV
P_V = P_K + C_RWKV
P_LORA = P_V + C_RWKV
P_KATT = P_LORA + LORA_W
P_VATT = P_KATT + C_KV
P_COLS = 12288
RWKV_PACK = 3 * C_RWKV + LORA_W

SUBLANES = 8
LANES = 128
TOK_TILE = 832
PROMPT_TILE = 1024
MIX_TILE = 128
PREP_TILE = 256
PEER_TILE = 128
G_PITCH = 136
EXPERT_TILE = 1024
SCAN_CHUNK = 64
CAST_ROWS = 1024
N_SCAN_KEYS = 5

_VMEM_LIMIT = 56 * 1024 * 1024

_f32 = jnp.float32
_bf16 = jnp.bfloat16


def _params(sem, vmem=None):
    return pltpu.CompilerParams(dimension_semantics=sem, vmem_limit_bytes=vmem)


def _rmsnorm_kernel(x_ref, g_ref, o_ref):
    x = x_ref[...]
    y = x * lax.rsqrt(jnp.mean(x * x, axis=-1, keepdims=True) + NORM_EPS)
    o_ref[...] = (y * g_ref[...]).astype(o_ref.dtype)


def _rmsnorm(x, g, out_dtype, tile):
    n, d = x.shape
    return pl.pallas_call(
        _rmsnorm_kernel,
        grid=(n // tile,),
        in_specs=[pl.BlockSpec((tile, d), lambda i: (i, 0)),
                  pl.BlockSpec((1, d), lambda i: (0, 0))],
        out_specs=pl.BlockSpec((tile, d), lambda i: (i, 0)),
        out_shape=jax.ShapeDtypeStruct((n, d), out_dtype),
        compiler_params=_params(("parallel",), _VMEM_LIMIT),
        name="rmsnorm",
    )(x, g.reshape(1, d))


def _add_rmsnorm_kernel(x_ref, p_ref, g_ref, s_ref, o_ref):
    x = x_ref[...] + p_ref[...]
    s_ref[...] = x
    y = x * lax.rsqrt(jnp.mean(x * x, axis=-1, keepdims=True) + NORM_EPS)
    o_ref[...] = (y * g_ref[...]).astype(o_ref.dtype)


def _add_rmsnorm(x, p, g, out_dtype, tile):
    n, d = x.shape
    big = pl.BlockSpec((tile, d), lambda i: (i, 0))
    return pl.pallas_call(
        _add_rmsnorm_kernel,
        grid=(n // tile,),
        in_specs=[big, big, pl.BlockSpec((1, d), lambda i: (0, 0))],
        out_specs=[big, big],
        out_shape=[jax.ShapeDtypeStruct((n, d), _f32), jax.ShapeDtypeStruct((n, d), out_dtype)],
        compiler_params=_params(("parallel",), _VMEM_LIMIT),
        name="add_rmsnorm",
    )(x, p, g.reshape(1, d))


def _mm_kernel(a_ref, b_ref, o_ref):
    o_ref[...] = jnp.dot(a_ref[...], b_ref[...], preferred_element_type=_f32).astype(o_ref.dtype)


def _mm_res_kernel(a_ref, b_ref, r_ref, o_ref):
    o_ref[...] = r_ref[...] + jnp.dot(a_ref[...], b_ref[...], preferred_element_type=_f32)


def _matmul(a, b, tm, tn, out_dtype, name, res=None):
    m, k = a.shape
    n = b.shape[1]
    in_specs = [pl.BlockSpec((tm, k), lambda i, j: (i, 0)),
                pl.BlockSpec((k, tn), lambda i, j: (0, j))]
    args = [a, b]
    kern = _mm_kernel
    if res is not None:
        in_specs.append(pl.BlockSpec((tm, tn), lambda i, j: (i, j)))
        args.append(res)
        kern = _mm_res_kernel
    return pl.pallas_call(
        kern,
        grid=(m // tm, n // tn),
        in_specs=in_specs,
        out_specs=pl.BlockSpec((tm, tn), lambda i, j: (i, j)),
        out_shape=jax.ShapeDtypeStruct((m, n), out_dtype),
        compiler_params=_params(("parallel", "parallel"), _VMEM_LIMIT),
        name=name,
    )(*args)


def _attn_prompt_kernel(sink_ref, q_ref, kp_ref, kc_ref, vp_ref, vc_ref, o_ref):
    n = pl.program_id(1)
    q = q_ref[...].astype(_bf16)
    kband = jnp.concatenate([kp_ref[...], kc_ref[...]], axis=0).astype(_bf16)
    vband = jnp.concatenate([vp_ref[...], vc_ref[...]], axis=0).astype(_bf16)
    row = lax.broadcasted_iota(jnp.int32, (WINDOW, 2 * WINDOW), 0)
    col = lax.broadcasted_iota(jnp.int32, (WINDOW, 2 * WINDOW), 1)
    first_col = jnp.where(n > 0, 0, WINDOW)
    valid = (col <= row + WINDOW) & (col >= row) & (col >= first_col)
    outs = []
    for h in range(N_Q_HEADS):
        g = h // Q_PER_KV
        qh = q[:, h * HEAD_DIM:(h + 1) * HEAD_DIM]
        kg = kband[:, g * HEAD_DIM:(g + 1) * HEAD_DIM]
        vg = vband[:, g * HEAD_DIM:(g + 1) * HEAD_DIM]
        s = lax.dot_general(qh, kg, (((1,), (1,)), ((), ())), preferred_element_type=_f32)
        s = jnp.where(valid, s * (HEAD_DIM ** -0.5), MASK_VALUE)
        sink = sink_ref[h]
        m = jnp.maximum(jnp.max(s, axis=-1, keepdims=True), sink)
        p = jnp.exp(s - m)
        denom = jnp.sum(p, axis=-1, keepdims=True) + jnp.exp(sink - m)
        o = jnp.dot(p.astype(_bf16), vg, preferred_element_type=_f32)
        outs.append(o / denom)
    o_ref[...] = jnp.concatenate(outs, axis=-1).astype(o_ref.dtype)


def _attn_prompt(h, sinks):
    nb = SEQ // WINDOW
    qb, kb, vb = P_Q // C_ATT, P_KATT // C_KV, P_VATT // C_KV
    cur = lambda col: (lambda b, n: (b * nb + n, col))
    prev = lambda col: (lambda b, n: (b * nb + jnp.maximum(n - 1, 0), col))
    return pl.pallas_call(
        _attn_prompt_kernel,
        grid=(BATCH, nb),
        in_specs=[pl.BlockSpec(memory_space=pltpu.SMEM),
                  pl.BlockSpec((WINDOW, C_ATT), cur(qb)),
                  pl.BlockSpec((WINDOW, C_KV), prev(kb)),
                  pl.BlockSpec((WINDOW, C_KV), cur(kb)),
                  pl.BlockSpec((WINDOW, C_KV), prev(vb)),
                  pl.BlockSpec((WINDOW, C_KV), cur(vb))],
        out_specs=pl.BlockSpec((WINDOW, C_ATT), lambda b, n: (b * nb + n, 0)),
        out_shape=jax.ShapeDtypeStruct((N_PROMPT, C_ATT), _bf16),
        compiler_params=_params(("parallel", "parallel")),
        name="attn_prompt",
    )(sinks, h, h, h, h, h)


def _attn_sample_kernel(sink_ref, q_ref, kc_ref, vc_ref, kn_ref, vn_ref, o_ref):
    q = q_ref[...]
    kc = kc_ref[...].astype(_bf16)
    vc = vc_ref[...].astype(_bf16)
    kn = kn_ref[...].astype(_bf16).astype(_f32)
    vn = vn_ref[...].astype(_bf16).astype(_f32)
    hidx = lax.broadcasted_iota(jnp.int32, (1, Q_PER_KV, 1), 1)
    outs = []
    for g in range(N_KV_HEADS):
        sl = slice(g * HEAD_DIM, (g + 1) * HEAD_DIM)
        qg = q[:, g * Q_PER_KV:(g + 1) * Q_PER_KV, :].astype(_bf16)
        s = jnp.einsum("bhd,bld->bhl", qg, kc[:, :, sl], preferred_element_type=_f32)
        s_new = jnp.sum(qg.astype(_f32) * kn[:, None, sl], axis=-1, keepdims=True)
        scale = HEAD_DIM ** -0.5
        s = s * scale
        s_new = s_new * scale
        sink = jnp.zeros((1, Q_PER_KV, 1), _f32)
        for j in range(Q_PER_KV):
            sink = jnp.where(hidx == j, sink_ref[g * Q_PER_KV + j], sink)
        m = jnp.maximum(jnp.maximum(jnp.max(s, axis=-1, keepdims=True), s_new), sink)
        p = jnp.exp(s - m)
        p_new = jnp.exp(s_new - m)
        denom = jnp.sum(p, axis=-1, keepdims=True) + p_new + jnp.exp(sink - m)
        o = jnp.einsum("bhl,bld->bhd", p.astype(_bf16), vc[:, :, sl], preferred_element_type=_f32)
        o = o + p_new.astype(_bf16).astype(_f32) * vn[:, None, sl]
        outs.append(o / denom)
    o_ref[...] = jnp.concatenate(outs, axis=1).astype(o_ref.dtype)


def _attn_sample(q3, kc, vc, kn, vn, sinks):
    tb = 16
    L = kc.shape[1]
    return pl.pallas_call(
        _attn_sample_kernel,
        grid=(DEC_BATCH // tb,),
        in_specs=[pl.BlockSpec(memory_space=pltpu.SMEM),
                  pl.BlockSpec((tb, N_Q_HEADS, HEAD_DIM), lambda i: (i, 0, 0)),
                  pl.BlockSpec((tb, L, C_KV), lambda i: (i, 0, 0)),
                  pl.BlockSpec((tb, L, C_KV), lambda i: (i, 0, 0)),
                  pl.BlockSpec((tb, C_KV), lambda i: (i, 0)),
                  pl.BlockSpec((tb, C_KV), lambda i: (i, 0))],
        out_specs=pl.BlockSpec((tb, N_Q_HEADS, HEAD_DIM), lambda i: (i, 0, 0)),
        out_shape=jax.ShapeDtypeStruct((DEC_BATCH, N_Q_HEADS, HEAD_DIM), _bf16),
        compiler_params=_params(("parallel",)),
        name="attn_sample",
    )(sinks, q3, kc, vc, kn, vn)


def _pool_project(d_groups, pw_ref, scale_ref, o_ref):
    outs = []
    for gi in range(POOL_GROUPS):
        outs.append(jnp.dot(d_groups[gi].astype(_bf16), pw_ref[gi], preferred_element_type=_f32))
    o_ref[...] = (jnp.concatenate(outs, axis=-1) * scale_ref[...]).astype(o_ref.dtype)


def _pool_prompt_kernel(zp_ref, zc_ref, pw_ref, scale_ref, o_ref, ze_scr):
    n = pl.program_id(1)
    tt = zc_ref.shape[0]
    halo = 2 * SUBLANES
    prev_tail = zp_ref[pl.ds(tt - halo, halo), :]
    keep_rows = jnp.where(n > 0, halo, 0)
    ze_scr[pl.ds(0, halo), :] = jnp.where(
        lax.broadcasted_iota(jnp.int32, prev_tail.shape, 0) < keep_rows, prev_tail, 0.0)
    ze_scr[pl.ds(halo, tt), :] = zc_ref[...]
    pos = n * tt + lax.broadcasted_iota(jnp.int32, (tt, 1), 0)
    d_groups = []
    for gi, w in enumerate(POOL_WINDOWS):
        cs = slice(gi * POOL_GROUP_WIDTH, (gi + 1) * POOL_GROUP_WIDTH)
        z = ze_scr[pl.ds(halo, tt), cs]
        win = z
        for s in range(1, w):
            win = win + ze_scr[pl.ds(halo - s, tt), cs]
        cnt = jnp.minimum(w, pos + 1).astype(_f32)
        d_groups.append(win / cnt - z)
    _pool_project(d_groups, pw_ref, scale_ref, o_ref)


def _pool_prompt(h, pool_w, pool_scale):
    tt = MIX_TILE
    nt = SEQ // tt
    cb = P_POOL // C_POOL
    return pl.pallas_call(
        _pool_prompt_kernel,
        grid=(BATCH, nt),
        in_specs=[pl.BlockSpec((tt, C_POOL), lambda b, n: (b * nt + jnp.maximum(n - 1, 0), cb)),
                  pl.BlockSpec((tt, C_POOL), lambda b, n: (b * nt + n, cb)),
                  pl.BlockSpec((POOL_GROUPS, POOL_GROUP_WIDTH, POOL_GROUP_WIDTH), lambda b, n: (0, 0, 0)),
                  pl.BlockSpec((1, C_POOL), lambda b, n: (0, 0))],
        out_specs=pl.BlockSpec((tt, C_POOL), lambda b, n: (b * nt + n, 0)),
        out_shape=jax.ShapeDtypeStruct((N_PROMPT, C_POOL), _bf16),
        scratch_shapes=[pltpu.VMEM((tt + 2 * SUBLANES, C_POOL), _f32)],
        compiler_params=_params(("parallel", "parallel")),
        name="pool_prompt",
    )(h, h, pool_w, pool_scale)


def _pool_sample_kernel(past_ref, z_ref, pw_ref, scale_ref, o_ref):
    z = z_ref[...]
    d_groups = []
    for gi, w in enumerate(POOL_WINDOWS):
        cs = slice(gi * POOL_GROUP_WIDTH, (gi + 1) * POOL_GROUP_WIDTH)
        zg = z[:, cs]
        win = zg
        for s in range(1, w):
            win = win + past_ref[POOL_BUF - s][:, cs]
        d_groups.append(win / float(w) - zg)
    _pool_project(d_groups, pw_ref, scale_ref, o_ref)


def _pool_sample(past_t, z, pool_w, pool_scale):
    vm = pl.BlockSpec(memory_space=pltpu.VMEM)
    return pl.pallas_call(
        _pool_sample_kernel,
        in_specs=[vm, vm, vm, vm],
        out_specs=vm,
        out_shape=jax.ShapeDtypeStruct((DEC_BATCH, C_POOL), _bf16),
        name="pool_sample",
    )(past_t, z, pool_w, pool_scale)


def _head_sums(x):
    s = x
    shift = RWKV_HEADS
    while shift < C_RWKV:
        s = s + pltpu.roll(s, shift, 1)
        shift *= 2
    return s


def _rwkv_prep_math(r, k, v, lo, w0_ref, a0_ref, kk_w_ref, ka_ref, rk_ref, w2_ref, a2_ref, g2_ref):
    lane = lax.broadcasted_iota(jnp.int32, lo.shape, 1)
    f = jnp.where(lane < W_LORA, jnp.tanh(lo),
                  jnp.where(lane < W_LORA + A_LORA, lo, jax.nn.sigmoid(lo))).astype(_bf16)
    wpre = w0_ref[...] + jnp.dot(f, w2_ref[...], preferred_element_type=_f32)
    neg = -wpre
    softplus = jnp.maximum(neg, 0.0) + jnp.log(1.0 + jnp.exp(-jnp.abs(neg)))
    w_log = -softplus - 0.5
    decay = jnp.exp(-jnp.exp(w_log))
    a = jax.nn.sigmoid(a0_ref[...] + jnp.dot(f, a2_ref[...], preferred_element_type=_f32))
    g = jnp.dot(f, g2_ref[...], preferred_element_type=_f32)
    kk = k * kk_w_ref[...]
    kk = kk / jnp.maximum(jnp.sqrt(_head_sums(kk * kk)), 1e-12)
    k_mod = k * (1.0 + (a - 1.0) * ka_ref[...])
    bonus = _head_sums(r * k_mod * rk_ref[...]) * v
    return (decay, kk, kk * a, k_mod, r, v), (bonus, g)


def _rwkv_prep_prompt_kernel(pr_ref, pk_ref, pv_ref, pl_ref, tr_ref, tk_ref, tv_ref, tl_ref,
                             mu_ref, w0_ref, a0_ref, kk_w_ref, ka_ref, rk_ref, w2_ref, a2_ref, g2_ref,
                             keys_ref, val_ref, bonus_ref, g_ref):
    tt = pr_ref.shape[0]
    keep = jnp.where(lax.rem(pl.program_id(0), SEQ // tt) != 0, 1.0, 0.0)
    first = lax.broadcasted_iota(jnp.int32, (tt, 1), 0) == 0

    def shift(p_ref, t_ref, off, width):
        p = p_ref[...]
        last = t_ref[pl.ds(SUBLANES - 1, 1), :] * keep
        prev = jnp.where(first, last, pltpu.roll(p, 1, 0))
        return p + (prev - p) * mu_ref[:, off:off + width]

    scan_in, (bonus, g) = _rwkv_prep_math(
        shift(pr_ref, tr_ref, 0, C_RWKV), shift(pk_ref, tk_ref, C_RWKV, C_RWKV),
        shift(pv_ref, tv_ref, 2 * C_RWKV, C_RWKV), shift(pl_ref, tl_ref, 3 * C_RWKV, LORA_W),
        w0_ref, a0_ref, kk_w_ref, ka_ref, rk_ref, w2_ref, a2_ref, g2_ref)
    tiles = lambda x: x.T.reshape(RWKV_HEAD_SIZE, RWKV_HEADS, tt)
    for a in range(N_SCAN_KEYS):
        keys_ref[a] = tiles(scan_in[a])
    val_ref[...] = tiles(scan_in[N_SCAN_KEYS])
    bonus_ref[...] = bonus
    g_ref[...] = g


def _rwkv_prep_sample_kernel(pr_ref, pk_ref, pv_ref, pl_ref, qr_ref, qk_ref, qv_ref, ql_ref,
                             mu_ref, w0_ref, a0_ref, kk_w_ref, ka_ref, rk_ref, w2_ref, a2_ref, g2_ref,
                             *outs):
    def shift(p_ref, q_ref, off, width):
        p = p_ref[...]
        return p + (q_ref[...] - p) * mu_ref[:, off:off + width]

    scan_in, post_in = _rwkv_prep_math(
        shift(pr_ref, qr_ref, 0, C_RWKV), shift(pk_ref, qk_ref, C_RWKV, C_RWKV),
        shift(pv_ref, qv_ref, 2 * C_RWKV, C_RWKV), shift(pl_ref, ql_ref, 3 * C_RWKV, LORA_W),
        w0_ref, a0_ref, kk_w_ref, ka_ref, rk_ref, w2_ref, a2_ref, g2_ref)
    for o_ref, x in zip(outs, scan_in + post_in):
        o_ref[...] = x


def _rwkv_prep(h, shift_pad, params, prompt):
    tt = PREP_TILE if prompt else DEC_BATCH
    n = N_PROMPT if prompt else DEC_BATCH
    row0 = 0 if prompt else N_PROMPT // tt
    cb = P_R // C_RWKV
    main = lambda width, c: pl.BlockSpec((tt, width), lambda i: (row0 + i, c))
    const = lambda i: (0, 0)
    vec = pl.BlockSpec((1, C_RWKV), const)
    lw = pl.BlockSpec((LORA_W, C_RWKV), const)
    tok = pl.BlockSpec((tt, C_RWKV), lambda i: (i, 0))
    tok_shape = jax.ShapeDtypeStruct((n, C_RWKV), _f32)
    if prompt:
        per = tt // SUBLANES
        tps = SEQ // tt
        prev = lambda width, c: pl.BlockSpec((SUBLANES, width), lambda i: (jnp.maximum(i * per - 1, 0), c))
        prev_specs = [prev(C_RWKV, cb), prev(C_RWKV, cb + 1), prev(C_RWKV, cb + 2), prev(LORA_W, P_LORA // LORA_W)]
        prev_args = [h] * 4
        kern = _rwkv_prep_prompt_kernel
        lanes = BATCH * RWKV_HEADS
        out_specs = [pl.BlockSpec((N_SCAN_KEYS, RWKV_HEAD_SIZE, RWKV_HEADS, tt), lambda i: (0, 0, i // tps, i % tps)),
                     pl.BlockSpec((RWKV_HEAD_SIZE, RWKV_HEADS, tt), lambda i: (0, i // tps, i % tps)), tok, tok]
        out_shape = [jax.ShapeDtypeStruct((N_SCAN_KEYS, RWKV_HEAD_SIZE, lanes, SEQ), _f32),
                     jax.ShapeDtypeStruct((RWKV_HEAD_SIZE, lanes, SEQ), _f32), tok_shape, tok_shape]
    else:
        prev = lambda width, c: pl.BlockSpec((tt, width), lambda i: (i, c))
        prev_specs = [prev(C_RWKV, 0), prev(C_RWKV, 1), prev(C_RWKV, 2), prev(LORA_W, 3 * C_RWKV // LORA_W)]
        prev_args = [shift_pad] * 4
        kern = _rwkv_prep_sample_kernel
        out_specs = [tok] * (N_SCAN_KEYS + 3)
        out_shape = [tok_shape] * (N_SCAN_KEYS + 3)
    return pl.pallas_call(
        kern,
        grid=(n // tt,),
        in_specs=[main(C_RWKV, cb), main(C_RWKV, cb + 1), main(C_RWKV, cb + 2), main(LORA_W, P_LORA // LORA_W)]
        + prev_specs + [pl.BlockSpec((1, RWKV_PACK), const), vec, vec, vec, vec, vec, lw, lw, lw],
        out_specs=out_specs,
        out_shape=out_shape,
        compiler_params=_params(("parallel",), _VMEM_LIMIT),
        name="rwkv_prep_prompt" if prompt else "rwkv_prep_sample",
    )(h, h, h, h, *prev_args, *params)


KEY_PITCH = 72
VAL_PITCH = 40
RELAYOUT_STEPS = LANES


def _relayout_kernel(a_ref, o_ref, *, rows, pitch, fold):
    zero = jnp.zeros((RELAYOUT_STEPS, LANES), _f32)
    for q in range(rows, pitch):
        o_ref[pl.ds(q, RELAYOUT_STEPS, stride=pitch), :] = zero
    for q in range(rows):
        lo = a_ref[q]
        hi = a_ref[q + rows] if fold == 2 else lo
        o_ref[pl.ds(q, RELAYOUT_STEPS, stride=pitch), :] = jnp.concatenate([lo, hi], axis=0).T


def _relayout_keys(a):
    n_t = SEQ // RELAYOUT_STEPS
    kern = functools.partial(_relayout_kernel, rows=RWKV_HEAD_SIZE, pitch=KEY_PITCH, fold=1)
    return pl.pallas_call(
        kern,
        grid=(N_SCAN_KEYS, n_t),
        in_specs=[pl.BlockSpec((None, RWKV_HEAD_SIZE, BATCH * RWKV_HEADS, RELAYOUT_STEPS), lambda a, i: (a, 0, 0, i))],
        out_specs=pl.BlockSpec((None, RELAYOUT_STEPS * KEY_PITCH, LANES), lambda a, i: (a, i, 0)),
        out_shape=jax.ShapeDtypeStruct((N_SCAN_KEYS, SEQ * KEY_PITCH, LANES), _f32),
        compiler_params=_params(("parallel", "parallel")),
        name="relayout_keys",
    )(a)


def _relayout_values(a):
    n_t = SEQ // RELAYOUT_STEPS
    half = RWKV_HEAD_SIZE // 2
    kern = functools.partial(_relayout_kernel, rows=half, pitch=VAL_PITCH, fold=2)
    return pl.pallas_call(
        kern,
        grid=(n_t,),
        in_specs=[pl.BlockSpec((RWKV_HEAD_SIZE, BATCH * RWKV_HEADS, RELAYOUT_STEPS), lambda i: (0, 0, i))],
        out_specs=pl.BlockSpec((RELAYOUT_STEPS * VAL_PITCH, LANES), lambda i: (i, 0)),
        out_shape=jax.ShapeDtypeStruct((SEQ * VAL_PITCH, LANES), _f32),
        compiler_params=_params(("parallel",)),
        name="relayout_values",
    )(a)


def _wkv_prompt_kernel(w_ref, kk_ref, b_ref, k_ref, r_ref, v_ref, y_ref, z_ref, part_scr, sa_scr):
    @pl.when(pl.program_id(0) == 0)
    def _():
        z_ref[...] = jnp.zeros_like(z_ref)

    y_ref[...] = jnp.zeros_like(y_ref)
    half = RWKV_HEAD_SIZE // 2
    groups = RWKV_HEAD_SIZE // SUBLANES
    steps = v_ref.shape[0] // VAL_PITCH

    def fold(x):
        return jnp.sum(x.reshape(groups, SUBLANES, LANES), axis=0)

    def finish():
        acc = part_scr[pl.ds(0, half, stride=SUBLANES), :]
        for s in range(1, SUBLANES):
            acc = acc + part_scr[pl.ds(s, half, stride=SUBLANES), :]
        return acc

    def step(t, carry):
        kb = pl.multiple_of(t * KEY_PITCH, SUBLANES)
        vb = pl.multiple_of(t * VAL_PITCH, SUBLANES)
        key = lambda ref: ref[pl.ds(kb, RWKV_HEAD_SIZE), :]
        w, kk, bb, k, r = key(w_ref), key(kk_ref), key(b_ref), key(k_ref), key(r_ref)
        for il in range(half):
            part_scr[pl.ds(il * SUBLANES, SUBLANES), :] = fold(z_ref[il] * kk)
        sa_scr[...] = -finish()
        for il in range(half):
            z = z_ref[il] * w + sa_scr[pl.ds(il, 1), :] * bb + v_ref[pl.ds(vb + il, 1), :] * k
            z_ref[il] = z
            part_scr[pl.ds(il * SUBLANES, SUBLANES), :] = fold(z * r)
        y_ref[pl.ds(vb, half), :] = finish()
        return carry

    lax.fori_loop(0, steps, step, 0)


def _wkv_prompt(keys, v):
    tc = SCAN_CHUNK
    half = RWKV_HEAD_SIZE // 2
    key = lambda a: pl.BlockSpec((None, tc * KEY_PITCH, LANES), lambda i: (a, i, 0))
    val = pl.BlockSpec((tc * VAL_PITCH, LANES), lambda i: (i, 0))
    return pl.pallas_call(
        _wkv_prompt_kernel,
        grid=(SEQ // tc,),
        in_specs=[key(a) for a in range(N_SCAN_KEYS)] + [val],
        out_specs=[val, pl.BlockSpec((half, RWKV_HEAD_SIZE, LANES), lambda i: (0, 0, 0))],
        out_shape=[jax.ShapeDtypeStruct((SEQ * VAL_PITCH, LANES), _f32),
                   jax.ShapeDtypeStruct((half, RWKV_HEAD_SIZE, LANES), _f32)],
        scratch_shapes=[pltpu.VMEM((half * SUBLANES, LANES), _f32), pltpu.VMEM((half, LANES), _f32)],
        compiler_params=_params(("arbitrary",), _VMEM_LIMIT),
        name="wkv_prompt",
    )(*([keys] * N_SCAN_KEYS), v)


def _wkv_sample_kernel(s_ref, w_ref, kk_ref, b_ref, k_ref, r_ref, v_ref, y_ref, so_ref):
    s = s_ref[...]
    row = lambda ref: ref[...]
    eye = (lax.broadcasted_iota(jnp.int32, (RWKV_HEAD_SIZE, RWKV_HEAD_SIZE), 0)
           == lax.broadcasted_iota(jnp.int32, (RWKV_HEAD_SIZE, RWKV_HEAD_SIZE), 1))
    sa = -jnp.sum(s * row(kk_ref), axis=-1, keepdims=True)
    vcol = jnp.sum(jnp.where(eye, row(v_ref), 0.0), axis=-1, keepdims=True)
    s = s * row(w_ref) + sa * row(b_ref) + vcol * row(k_ref)
    so_ref[...] = s
    ycol = jnp.sum(s * row(r_ref), axis=-1, keepdims=True)
    y_ref[...] = jnp.sum(jnp.where(eye, ycol, 0.0), axis=-2, keepdims=True)


def _wkv_sample(s0, w, kk, b, k, r, v):
    tb = 8
    st = pl.BlockSpec((tb, RWKV_HEADS, RWKV_HEAD_SIZE, RWKV_HEAD_SIZE), lambda i: (i, 0, 0, 0))
    vec = pl.BlockSpec((tb, RWKV_HEADS, 1, RWKV_HEAD_SIZE), lambda i: (i, 0, 0, 0))
    return pl.pallas_call(
        _wkv_sample_kernel,
        grid=(DEC_BATCH // tb,),
        in_specs=[st] + [vec] * 6,
        out_specs=[vec, st],
        out_shape=[jax.ShapeDtypeStruct((DEC_BATCH, RWKV_HEADS, 1, RWKV_HEAD_SIZE), _f32),
                   jax.ShapeDtypeStruct(s0.shape, _f32)],
        compiler_params=_params(("parallel",)),
        name="wkv_sample",
    )(s0, w, kk, b, k, r, v)


def _rwkv_post_math(y, bonus, g, lw_ref, lb_ref):
    inv = 1.0 / RWKV_HEAD_SIZE
    mean = _head_sums(y) * inv
    c = y - mean
    var = _head_sums(c * c) * inv
    yn = c * lax.rsqrt(var + GN_EPS) * lw_ref[...] + lb_ref[...]
    return (yn + bonus) * g


def _rwkv_post_sample_kernel(y_ref, bonus_ref, g_ref, lw_ref, lb_ref, o_ref):
    o_ref[...] = _rwkv_post_math(y_ref[...], bonus_ref[...], g_ref[...], lw_ref, lb_ref).astype(o_ref.dtype)


def _rwkv_post_sample(y, bonus, g, ln_w, ln_b):
    vm = pl.BlockSpec(memory_space=pltpu.VMEM)
    return pl.pallas_call(
        _rwkv_post_sample_kernel,
        in_specs=[vm] * 5,
        out_specs=vm,
        out_shape=jax.ShapeDtypeStruct(y.shape, _bf16),
        name="rwkv_post_sample",
    )(y, bonus, g, ln_w, ln_b)


def _rwkv_post_prompt_kernel(y_ref, bonus_ref, g_ref, lw_ref, lb_ref, o_ref):
    half = RWKV_HEAD_SIZE // 2
    lanes = BATCH * RWKV_HEADS
    yt = [y_ref[pl.ds(il, RELAYOUT_STEPS, stride=VAL_PITCH), :].T for il in range(half)]
    for b in range(BATCH):
        rows = []
        for i in range(RWKV_HEAD_SIZE):
            ih, il = divmod(i, half)
            r0 = ih * lanes + b * RWKV_HEADS
            rows.append(yt[il][r0:r0 + RWKV_HEADS, :])
        y = jnp.concatenate(rows, axis=0).T
        o_ref[b] = _rwkv_post_math(y, bonus_ref[b], g_ref[b], lw_ref, lb_ref).astype(o_ref.dtype)


def _rwkv_post_prompt(y_lanes, bonus, g, ln_w, ln_b):
    tt = RELAYOUT_STEPS
    tok = pl.BlockSpec((BATCH, tt, C_RWKV), lambda i: (0, i, 0))
    vec = pl.BlockSpec((1, C_RWKV), lambda i: (0, 0))
    as3 = lambda a: a.reshape(BATCH, SEQ, C_RWKV)
    return pl.pallas_call(
        _rwkv_post_prompt_kernel,
        grid=(SEQ // tt,),
        in_specs=[pl.BlockSpec((tt * VAL_PITCH, LANES), lambda i: (i, 0)), tok, tok, vec, vec],
        out_specs=tok,
        out_shape=jax.ShapeDtypeStruct((BATCH, SEQ, C_RWKV), _bf16),
        compiler_params=_params(("parallel",), _VMEM_LIMIT),
        name="rwkv_post_prompt",
    )(y_lanes, as3(bonus), as3(g), ln_w, ln_b).reshape(N_PROMPT, C_RWKV)


def _merge_kernel(ya_ref, yr_ref, yp_ref, wa_ref, wr_ref, wp_ref,
                  g0_ref, g1_ref, g2_ref, b0_ref, b1_ref, b2_ref, o_ref):
    acc = None
    for y_ref, w_ref, g_ref, b_ref in ((ya_ref, wa_ref, g0_ref, b0_ref),
                                       (yr_ref, wr_ref, g1_ref, b1_ref),
                                       (yp_ref, wp_ref, g2_ref, b2_ref)):
        term = jax.nn.sigmoid(g_ref[...] + b_ref[...]) * jnp.dot(
            y_ref[...], w_ref[...], preferred_element_type=_f32)
        acc = term if acc is None else acc + term
    o_ref[...] = acc.astype(o_ref.dtype)


def _merge(ya, yr, yp, wa, wr, wp, h, b_gate, prompt):
    tm, tn = (PROMPT_TILE, 512) if prompt else (DEC_BATCH, 512)
    rows = N_PROMPT if prompt else DEC_BATCH
    row0 = 0 if prompt else N_PROMPT // tm
    nj = D_MODEL // tn
    ysp = pl.BlockSpec((tm, C_ATT), lambda i, j: (i, 0))
    wsp = pl.BlockSpec((C_ATT, tn), lambda i, j: (0, j))
    gsp = lambda br: pl.BlockSpec((tm, tn), lambda i, j: (row0 + i, br * nj + j))
    bsp = lambda br: pl.BlockSpec((1, tn), lambda i, j: (0, br * nj + j))
    in_specs = [ysp, ysp, ysp, wsp, wsp, wsp, gsp(0), gsp(1), gsp(2), bsp(0), bsp(1), bsp(2)]
    args = [ya, yr, yp, wa, wr, wp, h, h, h, b_gate, b_gate, b_gate]
    return pl.pallas_call(
        _merge_kernel,
        grid=(rows // tm, nj),
        in_specs=in_specs,
        out_specs=pl.BlockSpec((tm, tn), lambda i, j: (i, j)),
        out_shape=jax.ShapeDtypeStruct((rows, D_MODEL), _bf16),
        compiler_params=_params(("parallel", "parallel"), _VMEM_LIMIT),
        name="merge_prompt" if prompt else "merge_sample",
    )(*args)


_CAND_R2 = tuple(min(PEER_TOPK, PEER_TOPK // (r1 + 1)) for r1 in range(PEER_TOPK))
_CAND_ROW0 = tuple(sum(_CAND_R2[:r1]) for r1 in range(PEER_TOPK))
_N_CAND = sum(_CAND_R2)
_CAND_ROWS = -(-_N_CAND // SUBLANES) * SUBLANES


def _peer_route_kernel(q_ref, keys_ref, i_ref, j_ref, g_ref,
                       t_scr, ti_scr, cand_scr, cid_scr, bi_scr, bj_scr, bg_scr):
    tp = q_ref.shape[0]
    q = q_ref[...]
    key_iota = lax.broadcasted_iota(jnp.int32, (N_KEYS, tp), 0).astype(_f32)
    cand_iota = lax.broadcasted_iota(jnp.int32, (_CAND_ROWS, tp), 0).astype(_f32)
    neg_inf = -jnp.inf
    for hd in range(PEER_HEADS):
        for c in range(2):
            off = (hd * 2 + c) * PEER_HALF
            s = lax.dot_general(keys_ref[c], q[:, off:off + PEER_HALF],
                                (((1,), (1,)), ((), ())), preferred_element_type=_f32)
            for rk in range(PEER_TOPK):
                m = jnp.max(s, axis=0, keepdims=True)
                at_max = jnp.where(s == m, key_iota, float(N_KEYS))
                pos = jnp.min(at_max, axis=0, keepdims=True)
                t_scr[pl.ds(c * PEER_TOPK + rk, 1), :] = m
                ti_scr[pl.ds(c * PEER_TOPK + rk, 1), :] = pos
                s = jnp.where(at_max == pos, neg_inf, s)
        cand_scr[pl.ds(_CAND_ROWS - SUBLANES, SUBLANES), :] = jnp.full((SUBLANES, tp), neg_inf, _f32)
        cid_scr[pl.ds(_CAND_ROWS - SUBLANES, SUBLANES), :] = jnp.zeros((SUBLANES, tp), _f32)
        for r1 in range(PEER_TOPK):
            cnt = _CAND_R2[r1]
            cand_scr[pl.ds(_CAND_ROW0[r1], cnt), :] = t_scr[pl.ds(r1, 1), :] + t_scr[pl.ds(PEER_TOPK, cnt), :]
            cid_scr[pl.ds(_CAND_ROW0[r1], cnt), :] = (ti_scr[pl.ds(r1, 1), :] * float(N_KEYS)
                                                      + ti_scr[pl.ds(PEER_TOPK, cnt), :])
        cand = cand_scr[...]
        cid = cid_scr[...]
        base = hd * PEER_TOPK
        for rk in range(PEER_TOPK):
            m = jnp.max(cand, axis=0, keepdims=True)
            at_max = jnp.where(cand == m, cand_iota, float(_CAND_ROWS))
            pos = jnp.min(at_max, axis=0, keepdims=True)
            sel = at_max == pos
            eid = jnp.max(jnp.where(sel, cid, -1.0), axis=0, keepdims=True).astype(jnp.int32)
            bg_scr[pl.ds(base + rk, 1), :] = m
            bi_scr[pl.ds(base + rk, 1), :] = eid >> 7
            bj_scr[pl.ds(base + rk, 1), :] = eid & (N_KEYS - 1)
            cand = jnp.where(sel, neg_inf, cand)
        best = bg_scr[pl.ds(base, PEER_TOPK), :]
        e = jnp.exp(best - best[0:1, :])
        bg_scr[pl.ds(base, PEER_TOPK), :] = e / jnp.sum(e, axis=0, keepdims=True)
    i_ref[...] = bi_scr[...].T
    j_ref[...] = bj_scr[...].T
    g_ref[...] = bg_scr[...].T


def _peer_route(q, keys):
    tp = PEER_TILE
    n = q.shape[0]
    slots = PEER_HEADS * PEER_TOPK
    osp = pl.BlockSpec((tp, slots), lambda i: (i, 0))
    return pl.pallas_call(
        _peer_route_kernel,
        grid=(n // tp,),
        in_specs=[pl.BlockSpec((tp, q.shape[1]), lambda i: (i, 0)),
                  pl.BlockSpec((2, N_KEYS, PEER_HALF), lambda i: (0, 0, 0))],
        out_specs=[osp, osp, osp],
        out_shape=[jax.ShapeDtypeStruct((n, slots), jnp.int32),
                   jax.ShapeDtypeStruct((n, slots), jnp.int32),
                   jax.ShapeDtypeStruct((n, slots), _f32)],
        scratch_shapes=[pltpu.VMEM((2 * PEER_TOPK, tp), _f32),
                        pltpu.VMEM((2 * PEER_TOPK, tp), _f32),
                        pltpu.VMEM((_CAND_ROWS, tp), _f32),
                        pltpu.VMEM((_CAND_ROWS, tp), _f32),
                        pltpu.VMEM((slots, tp), jnp.int32),
                        pltpu.VMEM((slots, tp), jnp.int32),
                        pltpu.VMEM((slots, tp), _f32)],
        compiler_params=_params(("parallel",)),
        name="peer_route",
    )(q, keys)


def _peer_gates_kernel(i_ref, j_ref, g_ref, o_ref, tile_scr):
    tp = i_ref.shape[0]
    slots = i_ref.shape[1]
    key_iota = lax.broadcasted_iota(jnp.int32, (N_KEYS, slots), 0)

    def per_token(p, carry):
        irow = i_ref[pl.ds(p, 1), :]
        jrow = j_ref[pl.ds(p, 1), :]
        grow = g_ref[pl.ds(p, 1), :]
        at = jnp.where(key_iota == irow, grow, 0.0).astype(_bf16)
        bt = jnp.where(key_iota == jrow, 1.0, 0.0).astype(_bf16)
        tile = lax.dot_general(at, bt, (((1,), (1,)), ((), ())), preferred_element_type=_f32)
        tile_scr[pl.ds(pl.multiple_of(p * G_PITCH, SUBLANES), N_KEYS), :] = tile
        return carry

    lax.fori_loop(0, tp, per_token, 0, unroll=64)
    for i in range(N_KEYS):
        o_ref[:, i * N_KEYS:(i + 1) * N_KEYS] = tile_scr[pl.ds(i, tp, stride=G_PITCH), :].astype(o_ref.dtype)


def _peer_gates(isel, jsel, gate):
    tp = PEER_TILE
    n, slots = isel.shape
    isp = pl.BlockSpec((tp, slots), lambda i: (i, 0))
    return pl.pallas_call(
        _peer_gates_kernel,
        grid=(n // tp,),
        in_specs=[isp, isp, isp],
        out_specs=pl.BlockSpec((tp, N_EXPERTS), lambda i: (i, 0)),
        out_shape=jax.ShapeDtypeStruct((n, N_EXPERTS), _bf16),
        scratch_shapes=[pltpu.VMEM((tp * G_PITCH, N_KEYS), _f32)],
        compiler_params=_params(("parallel",), _VMEM_LIMIT),
        name="peer_gates",
    )(isel, jsel, gate)


def _peer_dense_kernel(x_ref, u_ref, v_ref, g_ref, o_ref):
    @pl.when(pl.program_id(1) == 0)
    def _():
        o_ref[...] = jnp.zeros_like(o_ref)

    s = lax.dot_general(x_ref[...], u_ref[...], (((1,), (1,)), ((), ())), preferred_element_type=_f32)
    act = 0.5 * s * (1.0 + lax.erf(s * (2.0 ** -0.5)))
    wgt = (g_ref[...].astype(_f32) * act).astype(_bf16)
    o_ref[...] += jnp.dot(wgt, v_ref[...], preferred_element_type=_f32)


def _peer_dense(xn, u, v, gates):
    tm, te = TOK_TILE, EXPERT_TILE
    n, d = xn.shape
    return pl.pallas_call(
        _peer_dense_kernel,
        grid=(n // tm, N_EXPERTS // te),
        in_specs=[pl.BlockSpec((tm, d), lambda i, e: (i, 0)),
                  pl.BlockSpec((te, d), lambda i, e: (e, 0)),
                  pl.BlockSpec((te, d), lambda i, e: (e, 0)),
                  pl.BlockSpec((tm, te), lambda i, e: (i, e))],
        out_specs=pl.BlockSpec((tm, d), lambda i, e: (i, 0)),
        out_shape=jax.ShapeDtypeStruct((n, d), _f32),
        compiler_params=_params(("parallel", "arbitrary"), _VMEM_LIMIT),
        name="peer_dense",
    )(xn, u, v, gates)


def _cast_kernel(x_ref, o_ref):
    o_ref[...] = x_ref[...].astype(o_ref.dtype)


def _to_bf16(x, layer):
    _, rows, cols = x.shape
    tr = CAST_ROWS
    return pl.pallas_call(
        _cast_kernel,
        grid=(rows // tr,),
        in_specs=[pl.BlockSpec((None, tr, cols), lambda i: (layer, i, 0))],
        out_specs=pl.BlockSpec((tr, cols), lambda i: (i, 0)),
        out_shape=jax.ShapeDtypeStruct((rows, cols), _bf16),
        compiler_params=_params(("parallel",), _VMEM_LIMIT),
        name="to_bf16",
    )(x)


def _to_jh(x, axis=-1):
    axis = axis % x.ndim
    shape = x.shape
    x = x.reshape(shape[:axis] + (RWKV_HEADS, RWKV_HEAD_SIZE) + shape[axis + 1:])
    return jnp.swapaxes(x, axis, axis + 1).reshape(shape)


def _from_jh(x, axis=-1):
    axis = axis % x.ndim
    shape = x.shape
    x = x.reshape(shape[:axis] + (RWKV_HEAD_SIZE, RWKV_HEADS) + shape[axis + 1:])
    return jnp.swapaxes(x, axis, axis + 1).reshape(shape)


def _permute_rkv(x, perm):
    segs = [perm(x[..., s * C_RWKV:(s + 1) * C_RWKV]) for s in range(3)]
    return jnp.concatenate(segs + [x[..., 3 * C_RWKV:]], axis=-1)


def _pack_w_in(w):
    seg = lambda a, b: w[:, a:b]
    lora = seg(OFF_RWKV + 3 * C_RWKV, OFF_POOL)
    pad = lambda x, width: jnp.pad(x, ((0, 0), (0, width - x.shape[1])))
    packed = jnp.concatenate([
        seg(OFF_GATE, OFF_GATE + N_BRANCH * D_MODEL),
        seg(0, OFF_K),
        seg(OFF_POOL, OFF_GATE),
        _permute_rkv(seg(OFF_RWKV, OFF_RWKV + 3 * C_RWKV), _to_jh),
        pad(lora, LORA_W),
        seg(OFF_K, OFF_V),
        seg(OFF_V, OFF_RWKV),
    ], axis=1)
    return pad(packed, P_COLS).astype(_bf16)


def _pad_rows(w, rows, at):
    w = _to_jh(w)
    return jnp.zeros((rows, w.shape[1]), w.dtype).at[at:at + w.shape[0]].set(w).astype(_bf16)


def _sample_heads(a):
    return a.reshape(DEC_BATCH, RWKV_HEAD_SIZE, RWKV_HEADS).transpose(0, 2, 1).reshape(
        DEC_BATCH, RWKV_HEADS, 1, RWKV_HEAD_SIZE)


def _state_from_lanes(z):
    half = RWKV_HEAD_SIZE // 2
    t = z.reshape(half, RWKV_HEAD_SIZE, 2, BATCH, RWKV_HEADS).transpose(3, 4, 2, 0, 1)
    return t.reshape(BATCH, RWKV_HEADS, RWKV_HEAD_SIZE, RWKV_HEAD_SIZE)


def _layer(x, peer_prev, lp, state):
    cache_k, cache_v, wkv0, shift0, pool0 = state
    row = lambda a: a.reshape(1, -1)

    if peer_prev is None:
        xn = _rmsnorm(x, lp["norm_mix"], _bf16, TOK_TILE)
    else:
        x, xn = _add_rmsnorm(x, peer_prev, lp["norm_mix"], _bf16, TOK_TILE // 2)
    h = _matmul(xn, lp["w_in"], TOK_TILE, 1024, _f32, "proj_in")
    hs = h[N_PROMPT:]
    tail = lambda rows, off, width: jnp.stack(
        [lax.slice(h, ((b + 1) * SEQ - rows, off), ((b + 1) * SEQ, off + width)) for b in range(BATCH)])

    ya_p = _attn_prompt(h, lp["sinks"])
    kn, vn = hs[:, P_KATT:P_KATT + C_KV], hs[:, P_VATT:P_VATT + C_KV]
    L = cache_k.shape[1]
    ya_s = _attn_sample(hs[:, P_Q:P_Q + C_ATT].reshape(DEC_BATCH, N_Q_HEADS, HEAD_DIM),
                        cache_k.reshape(DEC_BATCH, L, C_KV), cache_v.reshape(DEC_BATCH, L, C_KV),
                        kn, vn, lp["sinks"]).reshape(DEC_BATCH, C_ATT)
    kv_shape = (BATCH, WINDOW, N_KV_HEADS, HEAD_DIM)
    new_k_p = tail(WINDOW, P_KATT, C_KV).reshape(kv_shape)
    new_v_p = tail(WINDOW, P_VATT, C_KV).reshape(kv_shape)
    shape_kv = (DEC_BATCH, 1, N_KV_HEADS, HEAD_DIM)
    new_k_s = jnp.concatenate([cache_k, kn.reshape(shape_kv)], axis=1)[:, -L:]
    new_v_s = jnp.concatenate([cache_v, vn.reshape(shape_kv)], axis=1)[:, -L:]

    prep_params = (lp["mu"], row(lp["w0"]), row(lp["a0"]), row(lp["k_k"]), row(lp["k_a"]), row(lp["r_k"]),
                   lp["w2"], lp["a2"], lp["g2"])
    keys_p, val_p, bonus_p, g_p = _rwkv_prep(h, None, prep_params, prompt=True)
    shift_pad = jnp.pad(_permute_rkv(shift0, _to_jh), ((0, 0), (0, RWKV_PACK - RWKV_COLS)))
    w_s, kk_s, b_s, k_s, r_s, v_s, bonus_s, g_s = _rwkv_prep(h, shift_pad, prep_params, prompt=False)
    y_lanes, z_lanes = _wkv_prompt(_relayout_keys(keys_p), _relayout_values(val_p))
    sm = _sample_heads
    y_s, new_wkv_s = _wkv_sample(wkv0, sm(w_s), sm(kk_s), sm(b_s), sm(k_s), sm(r_s), sm(v_s))
    y_s = y_s.reshape(DEC_BATCH, RWKV_HEADS, RWKV_HEAD_SIZE).transpose(0, 2, 1).reshape(DEC_BATCH, C_RWKV)
    yr_p = _rwkv_post_prompt(y_lanes, bonus_p, g_p, row(lp["ln_w"]), row(lp["ln_b"]))
    yr_s = _rwkv_post_sample(y_s, bonus_s, g_s, row(lp["ln_w"]), row(lp["ln_b"]))
    new_wkv_p = _state_from_lanes(z_lanes)
    new_shift_p = _permute_rkv(tail(1, P_R, RWKV_COLS).reshape(BATCH, RWKV_COLS), _from_jh)
    new_shift_s = _permute_rkv(hs[:, P_R:P_R + RWKV_COLS], _from_jh)

    yp_p = _pool_prompt(h, lp["pool_w"], row(lp["pool_scale"]))
    zs = hs[:, P_POOL:P_POOL + C_POOL]
    yp_s = _pool_sample(pool0.transpose(1, 0, 2), zs, lp["pool_w"], row(lp["pool_scale"]))
    new_pool_p = tail(POOL_BUF, P_POOL, C_POOL)
    new_pool_s = jnp.concatenate([pool0, zs[:, None]], axis=1)[:, -POOL_BUF:]

    mats = (lp["w_att_o"], lp["w_rwkv_o"], lp["w_pool_o"])
    merged = jnp.concatenate([_merge(ya_p, yr_p, yp_p, *mats, h, row(lp["b_gate"]), prompt=True),
                              _merge(ya_s, yr_s, yp_s, *mats, h, row(lp["b_gate"]), prompt=False)], axis=0)
    x = _matmul(merged, lp["w_out"], TOK_TILE, 1024, _f32, "proj_out", res=x)

    xn2 = _rmsnorm(x, lp["norm_ffn"], _bf16, TOK_TILE)
    q = _matmul(xn2, lp["peer_w_query"], TOK_TILE, 1024, _bf16, "peer_query")
    isel, jsel, gate = _peer_route(q, lp["peer_sub_keys"])
    gates = _peer_gates(isel, jsel, gate)
    peer = _peer_dense(xn2, lp["peer_u"], lp["peer_v"], gates)

    st_p = (new_k_p, new_v_p, new_wkv_p, new_shift_p, new_pool_p)
    st_s = (new_k_s, new_v_s, new_wkv_s, new_shift_s, new_pool_s)
    return x, peer, st_p, st_s


def kernel(x_prompt, x_sample, cache_k, cache_v, state_wkv, state_shift, state_pool, norm_mix, w_in, b_gate, attn_sinks, rwkv_mu, rwkv_w0, rwkv_w2, rwkv_a0, rwkv_a2, rwkv_g2, rwkv_k_k, rwkv_k_a, rwkv_r_k, rwkv_ln_w, rwkv_ln_b, pool_w, pool_scale, w_att_o, w_rwkv_o, w_pool_o, w_out, norm_ffn, peer_w_query, peer_sub_keys, peer_u, peer_v, norm_final):
    x = jnp.concatenate([x_prompt.reshape(N_PROMPT, D_MODEL), x_sample.reshape(DEC_BATCH, D_MODEL)], axis=0)
    new_p, new_s = [], []
    peer = None
    bf = lambda a: a.astype(_bf16)
    for l in range(DEPTH):
        mu_packed = jnp.pad(_permute_rkv(rwkv_mu[l], _to_jh), (0, RWKV_PACK - RWKV_COLS)).reshape(1, RWKV_PACK)
        pc = lambda a: _to_jh(a.reshape(C_RWKV))
        lp = dict(
            norm_mix=norm_mix[l], w_in=_pack_w_in(w_in[l]), b_gate=b_gate[l], sinks=attn_sinks[l],
            mu=mu_packed, w0=pc(rwkv_w0[l]), a0=pc(rwkv_a0[l]), k_k=pc(rwkv_k_k[l]), k_a=pc(rwkv_k_a[l]),
            w2=_pad_rows(rwkv_w2[l], LORA_W, 0), a2=_pad_rows(rwkv_a2[l], LORA_W, W_LORA),
            g2=_pad_rows(rwkv_g2[l], LORA_W, W_LORA + A_LORA),
            r_k=pc(rwkv_r_k[l]), ln_w=pc(rwkv_ln_w[l]), ln_b=pc(rwkv_ln_b[l]),
            pool_w=bf(pool_w[l]), pool_scale=pool_scale[l],
            w_att_o=bf(w_att_o[l]), w_rwkv_o=bf(_to_jh(w_rwkv_o[l], axis=0)),
            w_pool_o=bf(w_pool_o[l]), w_out=bf(w_out[l]),
            norm_ffn=norm_ffn[l], peer_w_query=bf(peer_w_query[l]), peer_sub_keys=bf(peer_sub_keys[l]),
            peer_u=_to_bf16(peer_u, l), peer_v=_to_bf16(peer_v, l))
        state = (cache_k[l], cache_v[l], state_wkv[l], state_shift[l], state_pool[l])
        x, peer, st_p, st_s = _layer(x, peer, lp, state)
        new_p.append(st_p)
        new_s.append(st_s)
    _, y = _add_rmsnorm(x, peer, norm_final, _f32, TOK_TILE // 2)
    y_prompt = y[:N_PROMPT].reshape(BATCH, SEQ, D_MODEL)
    y_sample = y[N_PROMPT:].reshape(DEC_BATCH, 1, D_MODEL)
    stack = lambda sts, i: jnp.stack([st[i] for st in sts])
    return (y_prompt, y_sample,
            stack(new_p, 0), stack(new_p, 1), stack(new_p, 2), stack(new_p, 3), stack(new_p, 4),
            stack(new_s, 0), stack(new_s, 1), stack(new_s, 2), stack(new_s, 3), stack(new_s, 4))
```
